```python
import jax, jax.numpy as jnp
from jax import lax
import numpy as np

D_MODEL = 1024
BATCH = 16
SEQ = 2048
DEPTH = 2
DEC_BATCH = 32
DEC_SEQ = 32
PAST_LEN = 2048

CHUNK = 64
D_PLE = 256
D_MIX = D_MODEL
LRU_WIDTH = D_MIX // 2
LRU_BLOCKS = 4
LRU_BLOCK = LRU_WIDTH // LRU_BLOCKS
CONV_W = 4
LRU_C = 8.0
RWKV_WIDTH = D_MIX - LRU_WIDTH
RWKV_HEAD = 64
RWKV_HEADS = RWKV_WIDTH // RWKV_HEAD
W_LORA = 64
A_LORA = 64
G_LORA = 128
RWKV_COLS = 3 * RWKV_WIDTH + W_LORA + A_LORA + G_LORA
IN_COLS = 2 * LRU_WIDTH + RWKV_COLS
PEER_HEADS = 8
N_KEYS = 128
N_EXPERTS = N_KEYS * N_KEYS
PEER_QDIM = 256
PEER_HALF = PEER_QDIM // 2
PEER_TOPK = 16
PEER_BLOCK = 128
EPS = 1e-6
GN_EPS = 64e-5

kernel_name = 'hymba_rglru_rwkv7_peer_stream_step'


def rms_norm(x, g):
    xf = x.astype(jnp.float32)
    y = xf * lax.rsqrt(jnp.mean(xf * xf, axis=-1, keepdims=True) + EPS)
    return (y * g.astype(jnp.float32)).astype(x.dtype)


def rglru_mixer(xb, gb, conv_buf, h0, cw, cb, wa, ba, wx, bx, lam, first):
    B, T, W = xb.shape
    full = jnp.concatenate([conv_buf.astype(xb.dtype), xb], axis=1)
    xc = cb + sum(full[:, j:j + T] * cw[j] for j in range(CONV_W))
    new_buf = full[:, T:]
    xblk = xc.reshape(B, T, LRU_BLOCKS, LRU_BLOCK)
    r = jax.nn.sigmoid(jnp.einsum('btni,nij->btnj', xblk, wa).reshape(B, T, W) + ba)
    ig = jax.nn.sigmoid(jnp.einsum('btni,nij->btnj', xblk, wx).reshape(B, T, W) + bx)
    log_a = (-LRU_C * r * jax.nn.softplus(-lam)).astype(jnp.float32)
    a = jnp.exp(log_a)
    mult = jnp.sqrt(-jnp.expm1(2.0 * log_a))
    if first:
        mult = mult.at[:, 0].set(1.0)
    b = mult * (ig * xc).astype(jnp.float32)
    b = b.at[:, 0].add(a[:, 0] * h0.astype(jnp.float32))

    def comb(left, right):
        a_l, b_l = left
        a_r, b_r = right
        return a_l * a_r, a_r * b_l + b_r

    _, h = lax.associative_scan(comb, (a, b), axis=1)
    y = h.astype(xb.dtype) * jax.nn.gelu(gb)
    return y, new_buf, h[:, -1]


def rwkv7_mixer(z, shift_prev, S0, mu, w0, w2, a0, a2, g2, k_k, k_a, r_k, ln_w, ln_b):
    B, T, _ = z.shape
    f32 = jnp.float32
    z_prev = jnp.concatenate([shift_prev[:, None].astype(z.dtype), z[:, :-1]], axis=1)
    zs = z + (z_prev - z) * mu
    W = RWKV_WIDTH
    r, k, v, zw, za, zg = jnp.split(zs, [W, 2 * W, 3 * W, 3 * W + W_LORA, 3 * W + W_LORA + A_LORA], axis=-1)
    w_log = -jax.nn.softplus(-(w0 + jnp.tanh(zw) @ w2)) - 0.5
    decay = jnp.exp(-jnp.exp(w_log.astype(f32)))
    iclr = jax.nn.sigmoid(a0 + za @ a2)
    g = jax.nn.sigmoid(zg) @ g2

    def heads(t):
        return t.reshape(B, T, RWKV_HEADS, RWKV_HEAD)

    kk = heads(k * k_k).astype(f32)
    kk = kk / jnp.maximum(jnp.sqrt(jnp.sum(kk * kk, axis=-1, keepdims=True)), 1e-12)
    k = k * (1.0 + (iclr - 1.0) * k_a)
    rh, kh, vh, ah = heads(r), heads(k), heads(v), heads(iclr)
    wh = heads(decay)
    a_vec = -kk
    b_vec = kk * ah.astype(f32)

    def step(S, inp):
        r_t, w_t, k_t, v_t, a_t, b_t = inp
        sa = jnp.einsum('bhij,bhj->bhi', S, a_t)
        S = S * w_t[:, :, None, :] + sa[..., None] * b_t[:, :, None, :] + v_t[..., None] * k_t[:, :, None, :]
        return S, jnp.einsum('bhij,bhj->bhi', S, r_t)

    def tm(t):
        return jnp.moveaxis(t.astype(f32), 1, 0)

    S_T, o = lax.scan(step, S0.astype(f32), (tm(rh), tm(wh), tm(kh), tm(vh), tm(a_vec), tm(b_vec)))
    o = jnp.moveaxis(o, 0, 1)
    mean = jnp.mean(o, axis=-1, keepdims=True)
    var = jnp.mean(jnp.square(o - mean), axis=-1, keepdims=True)
    o = ((o - mean) * lax.rsqrt(var + GN_EPS)).reshape(B, T, W) * ln_w + ln_b
    bonus = jnp.sum((rh * kh * r_k).astype(f32), axis=-1, keepdims=True) * vh.astype(f32)
    o = (o + bonus.reshape(B, T, W)) * g
    return o.astype(z.dtype), z[:, -1], S_T


def peer_ffn(xn, wq, keys, u, v):
    B, T, D = xn.shape
    n = B * T
    nblk = -(-n // PEER_BLOCK)
    xf = jnp.pad(xn.reshape(n, D), ((0, nblk * PEER_BLOCK - n), (0, 0))).reshape(nblk, PEER_BLOCK, D)

    def block(xb):
        q = (xb @ wq).reshape(PEER_BLOCK, PEER_HEADS, 2, PEER_HALF)
        s = jnp.einsum('thsc,hskc->thsk', q, keys).astype(jnp.float32)
        top_s, top_i = lax.top_k(s, PEER_TOPK)
        cand = top_s[:, :, 0, :, None] + top_s[:, :, 1, None, :]
        best_s, best_c = lax.top_k(cand.reshape(PEER_BLOCK, PEER_HEADS, PEER_TOPK * PEER_TOPK), PEER_TOPK)
        i1 = jnp.take_along_axis(top_i[:, :, 0], best_c // PEER_TOPK, axis=-1)
        i2 = jnp.take_along_axis(top_i[:, :, 1], best_c % PEER_TOPK, axis=-1)
        e = (i1 * N_KEYS + i2).reshape(PEER_BLOCK, PEER_HEADS * PEER_TOPK)
        gate = jax.nn.softmax(best_s, axis=-1).reshape(PEER_BLOCK, PEER_HEADS * PEER_TOPK)
        hid = jnp.einsum('td,tkd->tk', xb, jnp.take(u, e, axis=0)).astype(jnp.float32)
        wgt = (gate * jax.nn.gelu(hid)).astype(xb.dtype)
        return jnp.einsum('tk,tkd->td', wgt, jnp.take(v, e, axis=0))

    out = lax.map(block, xf).reshape(nblk * PEER_BLOCK, D)[:n]
    return out.reshape(B, T, D)


def _trunk(x, p, conv_st, lru_st, shift_st, wkv_st, first, params):
    (norm1_g, w_in, conv_w, conv_b, lru_wa, lru_ba, lru_wx, lru_bx, lru_lambda,
     rwkv_mu, rwkv_w0, rwkv_w2, rwkv_a0, rwkv_a2, rwkv_g2, rwkv_kk, rwkv_ka, rwkv_rk,
     rwkv_ln_w, rwkv_ln_b, w_out, norm2_g, peer_wq, peer_keys, peer_u, peer_v,
     norm3_g, ple_w, ple_gate, final_g) = params
    h = x
    convs, lrus, shifts, wkvs = [], [], [], []
    for i in range(DEPTH):
        xn = rms_norm(h, norm1_g[i])
        zin = xn @ w_in[i]
        xb = zin[..., :LRU_WIDTH]
        gb = zin[..., LRU_WIDTH:2 * LRU_WIDTH]
        z = zin[..., 2 * LRU_WIDTH:]
        y_lru, cbuf, h_lru = rglru_mixer(xb, gb, conv_st[i], lru_st[i], conv_w[i], conv_b[i],
                                         lru_wa[i], lru_ba[i], lru_wx[i], lru_bx[i], lru_lambda[i], first)
        y_rw, sh, S = rwkv7_mixer(z, shift_st[i], wkv_st[i], rwkv_mu[i], rwkv_w0[i], rwkv_w2[i],
                                  rwkv_a0[i], rwkv_a2[i], rwkv_g2[i], rwkv_kk[i], rwkv_ka[i],
                                  rwkv_rk[i], rwkv_ln_w[i], rwkv_ln_b[i])
        h = h + jnp.concatenate([y_lru, y_rw], axis=-1) @ w_out[i]
        h = h + peer_ffn(rms_norm(h, norm2_g[i]), peer_wq[i], peer_keys[i], peer_u[i], peer_v[i])
        h = h + (p[i] @ ple_w[i]) * jax.nn.sigmoid(rms_norm(h, norm3_g[i]) @ ple_gate[i])
        convs.append(cbuf)
        lrus.append(h_lru)
        shifts.append(sh)
        wkvs.append(S)
    y = rms_norm(h, final_g)
    return (y, jnp.stack(convs).astype(x.dtype), jnp.stack(lrus).astype(x.dtype),
            jnp.stack(shifts).astype(x.dtype), jnp.stack(wkvs).astype(x.dtype))


def setup_inputs(seed: int = 0) -> dict:
    key = jax.random.key(seed)
    ks = iter(jax.random.split(key, 48))

    def nrm(shape, scale):
        return jax.random.normal(next(ks), shape, jnp.float32) * scale

    def unif(shape, lo, hi):
        return jax.random.uniform(next(ks), shape, jnp.float32, lo, hi)

    L = DEPTH
    a_base = unif((L, LRU_WIDTH), 0.9, 0.999)
    s = a_base ** (1.0 / LRU_C)
    lru_lambda = jnp.log(s) - jnp.log1p(-s)
    return {
        'x_prompt': nrm((BATCH, SEQ, D_MODEL), 1.0),
        'x_sample': nrm((DEC_BATCH, DEC_SEQ, D_MODEL), 1.0),
        'p_prompt': nrm((L, BATCH, SEQ, D_PLE), 1.0),
        'p_sample': nrm((L, DEC_BATCH, DEC_SEQ, D_PLE), 1.0),
        'state_conv': nrm((L, DEC_BATCH, CONV_W - 1, LRU_WIDTH), 1.0),
        'state_lru': nrm((L, DEC_BATCH, LRU_WIDTH), 0.5),
        'state_shift': nrm((L, DEC_BATCH, RWKV_COLS), 1.0),
        'state_wkv': nrm((L, DEC_BATCH, RWKV_HEADS, RWKV_HEAD, RWKV_HEAD), 1.0),
        'norm1_g': 1.0 + nrm((L, D_MODEL), 0.05),
        'w_in': nrm((L, D_MODEL, IN_COLS), D_MODEL ** -0.5),
        'conv_w': nrm((L, CONV_W, LRU_WIDTH), CONV_W ** -0.5),
        'conv_b': nrm((L, LRU_WIDTH), 0.02),
        'lru_wa': nrm((L, LRU_BLOCKS, LRU_BLOCK, LRU_BLOCK), LRU_BLOCK ** -0.5),
        'lru_ba': nrm((L, LRU_WIDTH), 0.02),
        'lru_wx': nrm((L, LRU_BLOCKS, LRU_BLOCK, LRU_BLOCK), LRU_BLOCK ** -0.5),
        'lru_bx': nrm((L, LRU_WIDTH), 0.02),
        'lru_lambda': lru_lambda,
        'rwkv_mu': unif((L, RWKV_COLS), 0.0, 1.0),
        'rwkv_w0': unif((L, RWKV_WIDTH), -6.0, -1.0),
        'rwkv_w2': nrm((L, W_LORA, RWKV_WIDTH), 0.1 * W_LORA ** -0.5),
        'rwkv_a0': nrm((L, RWKV_WIDTH), 0.1),
        'rwkv_a2': nrm((L, A_LORA, RWKV_WIDTH), 0.1 * A_LORA ** -0.5),
        'rwkv_g2': nrm((L, G_LORA, RWKV_WIDTH), G_LORA ** -0.5),
        'rwkv_kk': 0.85 + nrm((L, RWKV_WIDTH), 0.02),
        'rwkv_ka': 1.0 + nrm((L, RWKV_WIDTH), 0.02),
        'rwkv_rk': nrm((L, RWKV_HEADS, RWKV_HEAD), 0.1),
        'rwkv_ln_w': 1.0 + nrm((L, RWKV_WIDTH), 0.05),
        'rwkv_ln_b': nrm((L, RWKV_WIDTH), 0.02),
        'w_out': nrm((L, D_MIX, D_MODEL), D_MIX ** -0.5),
        'norm2_g': 1.0 + nrm((L, D_MODEL), 0.05),
        'peer_wq': nrm((L, D_MODEL, PEER_HEADS * PEER_QDIM), D_MODEL ** -0.5),
        'peer_keys': nrm((L, PEER_HEADS, 2, N_KEYS, PEER_HALF), PEER_HALF ** -0.5),
        'peer_u': nrm((L, N_EXPERTS, D_MODEL), D_MODEL ** -0.5),
        'peer_v': nrm((L, N_EXPERTS, D_MODEL), PEER_HEADS ** -0.5),
        'norm3_g': 1.0 + nrm((L, D_MODEL), 0.05),
        'ple_w': nrm((L, D_PLE, D_MODEL), D_PLE ** -0.5),
        'ple_gate': nrm((L, D_MODEL, D_MODEL), D_MODEL ** -0.5),
        'final_g': 1.0 + nrm((D_MODEL,), 0.05),
    }


def reference(x_prompt, x_sample, p_prompt, p_sample, state_conv, state_lru, state_shift, state_wkv,
              norm1_g, w_in, conv_w, conv_b, lru_wa, lru_ba, lru_wx, lru_bx, lru_lambda,
              rwkv_mu, rwkv_w0, rwkv_w2, rwkv_a0, rwkv_a2, rwkv_g2, rwkv_kk, rwkv_ka, rwkv_rk,
              rwkv_ln_w, rwkv_ln_b, w_out, norm2_g, peer_wq, peer_keys, peer_u, peer_v,
              norm3_g, ple_w, ple_gate, final_g):
    params = (norm1_g, w_in, conv_w, conv_b, lru_wa, lru_ba, lru_wx, lru_bx, lru_lambda,
              rwkv_mu, rwkv_w0, rwkv_w2, rwkv_a0, rwkv_a2, rwkv_g2, rwkv_kk, rwkv_ka, rwkv_rk,
              rwkv_ln_w, rwkv_ln_b, w_out, norm2_g, peer_wq, peer_keys, peer_u, peer_v,
              norm3_g, ple_w, ple_gate, final_g)
    bp = x_prompt.shape[0]
    dt = x_prompt.dtype
    zc = jnp.zeros((DEPTH, bp, CONV_W - 1, LRU_WIDTH), dt)
    zl = jnp.zeros((DEPTH, bp, LRU_WIDTH), dt)
    zs = jnp.zeros((DEPTH, bp, RWKV_COLS), dt)
    zw = jnp.zeros((DEPTH, bp, RWKV_HEADS, RWKV_HEAD, RWKV_HEAD), dt)
    y_prompt, conv_p, lru_p, shift_p, wkv_p = _trunk(x_prompt, p_prompt, zc, zl, zs, zw, True, params)
    y_sample, conv_s, lru_s, shift_s, wkv_s = _trunk(x_sample, p_sample, state_conv, state_lru,
                                                     state_shift, state_wkv, False, params)
    return (y_prompt, y_sample, conv_p, lru_p, shift_p, wkv_p, conv_s, lru_s, shift_s, wkv_s)
```

```python
import functools
import math

import jax
import jax.numpy as jnp
from jax import lax
from jax.experimental import pallas as pl
from jax.experimental.pallas import tpu as pltpu

F32 = jnp.float32
BF16 = jnp.bfloat16

D_MODEL = 1024
D_PLE = 256
LRU_WIDTH = 512
LRU_BLOCKS = 4
LRU_BLOCK = 128
CONV_W = 4
LRU_C = 8.0
RWKV_WIDTH = 512
RWKV_HEAD = 64
RWKV_HEADS = 8
RWKV_PAIRS = RWKV_HEADS // 2
LORA_WA = 128
G_LORA = 128
RWKV_COLS = 3 * RWKV_WIDTH + LORA_WA + G_LORA
LRU_COLS = 2 * LRU_WIDTH
PEER_HEADS = 8
N_KEYS = 128
N_EXPERTS = N_KEYS * N_KEYS
PEER_HALF = 128
PEER_TOPK = 16
EPS = 1e-6
GN_EPS = 64e-5

RWKV_CHUNK = 128
VMEM_LIMIT = 56 * 1024 * 1024


def _cparams(sem):
    return pltpu.CompilerParams(dimension_semantics=sem, vmem_limit_bytes=VMEM_LIMIT)


def _rms(x, g):
    return x * lax.rsqrt(jnp.mean(x * x, axis=-1, keepdims=True) + EPS) * g


def _gelu(x):
    c = math.sqrt(2.0 / math.pi)
    return x * (0.5 * (1.0 + jnp.tanh(c * (x + 0.044715 * (x * x * x)))))


def _sigmoid(x):
    return 1.0 / (1.0 + jnp.exp(-x))


def _softplus(x):
    return jnp.maximum(x, 0.0) + jnp.log1p(jnp.exp(-jnp.abs(x)))


def _dot(a, b):
    return jnp.dot(a, b, preferred_element_type=F32)


def _dot_nt(a, b):
    return lax.dot_general(a, b, (((1,), (1,)), ((), ())), preferred_element_type=F32)


def _dot_tn(a, b):
    return lax.dot_general(a, b, (((0,), (0,)), ((), ())), preferred_element_type=F32)


def _inproj_kernel(x_ref, g_ref, wl_ref, wr_ref, zl_ref, zr_ref):
    xn = _rms(x_ref[...], g_ref[...]).astype(BF16)
    zl_ref[...] = _dot(xn, wl_ref[...])
    zr_ref[...] = _dot(xn, wr_ref[...])


def _inproj(x, g, wl, wr, tm):
    B, T, D = x.shape
    return pl.pallas_call(
        _inproj_kernel,
        grid=(B, T // tm),
        in_specs=[
            pl.BlockSpec((None, tm, D), lambda b, i: (b, i, 0)),
            pl.BlockSpec((1, D), lambda b, i: (0, 0)),
            pl.BlockSpec((D, LRU_COLS), lambda b, i: (0, 0)),
            pl.BlockSpec((D, RWKV_COLS), lambda b, i: (0, 0)),
        ],
        out_specs=[
            pl.BlockSpec((tm, LRU_COLS), lambda b, i: (i, b)),
            pl.BlockSpec((tm, RWKV_COLS), lambda b, i: (i, b)),
        ],
        out_shape=[
            jax.ShapeDtypeStruct((T, B * LRU_COLS), F32),
            jax.ShapeDtypeStruct((T, B * RWKV_COLS), F32),
        ],
        compiler_params=_cparams(("parallel", "parallel")),
        name="inproj",
    )(x, g, wl, wr)


def _lru_kernel(z_ref, conv0_ref, h0_ref, cw_ref, cb_ref, wa_ref, ba_ref, wx_ref, bx_ref,
                lam_ref, y_ref, hlast_ref, hist_s, h_s, a_s, b_s, *, B, Lt, first):
    i = pl.program_id(0)
    W = LRU_WIDTH
    R = Lt * B

    @pl.when(i == 0)
    def _():
        hist_s[...] = conv0_ref[...]
        h_s[...] = h0_ref[...]

    xb = z_ref[:, 0:W]
    gb = z_ref[:, W:2 * W]
    full = jnp.concatenate([hist_s[...], xb], axis=0)
    xc = cb_ref[...] + full[0:R] * cw_ref[0:1, :]
    for j in range(1, CONV_W):
        xc = xc + full[j * B:j * B + R] * cw_ref[j:j + 1, :]
    hist_s[...] = full[R:R + (CONV_W - 1) * B]

    ra, ix = [], []
    for n in range(LRU_BLOCKS):
        xn = xc[:, n * LRU_BLOCK:(n + 1) * LRU_BLOCK].astype(BF16)
        ra.append(_dot(xn, wa_ref[n]))
        ix.append(_dot(xn, wx_ref[n]))
    r = _sigmoid(jnp.concatenate(ra, axis=1) + ba_ref[...])
    ig = _sigmoid(jnp.concatenate(ix, axis=1) + bx_ref[...])
    log_a = -LRU_C * r * _softplus(-lam_ref[...])
    a = jnp.exp(log_a)
    t = jnp.tanh(log_a)
    mult = jnp.sqrt(-2.0 * t / (1.0 - t))
    if first:
        row = lax.broadcasted_iota(jnp.int32, (R, 1), 0)
        mult = jnp.where(row < jnp.where(i == 0, B, 0), 1.0, mult)
    a_s[...] = a
    b_s[...] = mult * (ig * xc)

    def step(tt, h):
        r0 = pl.multiple_of(tt * B, B)
        h = a_s[pl.ds(r0, B), :] * h + b_s[pl.ds(r0, B), :]
        b_s[pl.ds(r0, B), :] = h
        return h

    h = lax.fori_loop(0, Lt, step, h_s[...], unroll=8)
    h_s[...] = h
    hlast_ref[...] = h
    y_ref[...] = b_s[...] * _gelu(gb)


def _lru(zl, conv0, h0, p, B, T, Lt, first):
    W = LRU_WIDTH
    R = Lt * B
    kern = functools.partial(_lru_kernel, B=B, Lt=Lt, first=first)
    full = lambda shape: pl.BlockSpec(shape, lambda i: (0,) * len(shape))
    return pl.pallas_call(
        kern,
        grid=(T // Lt,),
        in_specs=[
            pl.BlockSpec((R, LRU_COLS), lambda i: (i, 0)),
            full(((CONV_W - 1) * B, W)),
            full((B, W)),
            full((CONV_W, W)),
            full((1, W)),
            full((LRU_BLOCKS, LRU_BLOCK, LRU_BLOCK)),
            full((1, W)),
            full((LRU_BLOCKS, LRU_BLOCK, LRU_BLOCK)),
            full((1, W)),
            full((1, W)),
        ],
        out_specs=[
            pl.BlockSpec((R, W), lambda i: (i, 0)),
            full((B, W)),
        ],
        out_shape=[
            jax.ShapeDtypeStruct((T * B, W), F32),
            jax.ShapeDtypeStruct((B, W), F32),
        ],
        scratch_shapes=[
            pltpu.VMEM(((CONV_W - 1) * B, W), F32),
            pltpu.VMEM((B, W), F32),
            pltpu.VMEM((R, W), F32),
            pltpu.VMEM((R, W), F32),
        ],
        compiler_params=_cparams(("arbitrary",)),
        name="rglru",
    )(zl, conv0, h0, p["cw"], p["cb"], p["wa"], p["ba"], p["wx"], p["bx"], p["lam"])


def _rwkv_kernel(z_ref, sh0_ref, wt0_ref, mu_ref, w0_ref, w2_ref, a0_ref, a2_ref, g2_ref,
                 kk_ref, ka_ref, rk_ref, lnw_ref, lnb_ref, y_ref, wt_ref, zprev_s, wt_s,
                 *, Tv):
    L = RWKV_CHUNK
    Wd = RWKV_WIDTH
    c = pl.program_id(1)

    @pl.when(c == 0)
    def _():
        zprev_s[...] = sh0_ref[...]
        wt_s[...] = wt0_ref[...]

    z = z_ref[...]
    if Tv < L:
        z = jnp.concatenate([z, jnp.zeros((L - Tv, RWKV_COLS), F32)], axis=0)
    row = lax.broadcasted_iota(jnp.int32, (L, 1), 0)
    zp = jnp.where(row == 0, zprev_s[...], pltpu.roll(z, 1, 0))
    zprev_s[...] = z[Tv - 1:Tv, :]
    zs = z + (zp - z) * mu_ref[...]
    r = zs[:, 0:Wd]
    k = zs[:, Wd:2 * Wd]
    v = zs[:, 2 * Wd:3 * Wd]
    zwa = zs[:, 3 * Wd:3 * Wd + LORA_WA]
    zg = zs[:, 3 * Wd + LORA_WA:]
    if Tv < L:
        valid = row < Tv
        r = jnp.where(valid, r, 0.0)
        k = jnp.where(valid, k, 0.0)
        v = jnp.where(valid, v, 0.0)

    w_log = -_softplus(-(w0_ref[...] + _dot(jnp.tanh(zwa).astype(BF16), w2_ref[...]))) - 0.5
    ld = -jnp.exp(w_log)
    if Tv < L:
        ld = jnp.where(valid, ld, 0.0)
    iclr = _sigmoid(a0_ref[...] + _dot(zwa.astype(BF16), a2_ref[...]))
    g = _dot(_sigmoid(zg).astype(BF16), g2_ref[...])
    kkraw = k * kk_ref[...]
    kmod = k * (1.0 + (iclr - 1.0) * ka_ref[...])

    ri = lax.broadcasted_iota(jnp.int32, (L, L), 0)
    ci = lax.broadcasted_iota(jnp.int32, (L, L), 1)
    strict = ri > ci
    incl = ri >= ci
    tri = jnp.where(incl, 1.0, 0.0).astype(BF16)
    ld_hi = ld.astype(BF16)
    rem = ld - ld_hi.astype(F32)
    ld_mid = rem.astype(BF16)
    ld_lo = (rem - ld_mid.astype(F32)).astype(BF16)
    cum = _dot(tri, ld_hi) + _dot(tri, ld_mid) + _dot(tri, ld_lo)
    cum_l = cum[L - 1:L, :]
    cumm = cum - cum[L // 2 - 1:L // 2, :]
    e_m = jnp.exp(cumm)
    e_mi = jnp.exp(-cumm)
    e_mprev = jnp.exp(cumm - ld)
    e_0 = jnp.exp(cum)
    e_0prev = jnp.exp(cum - ld)
    e_end = jnp.exp(cum_l - cum)
    p_l = jnp.exp(cum_l)

    lane = lax.broadcasted_iota(jnp.int32, (1, 2 * RWKV_HEAD), 1)
    m0 = lane < RWKV_HEAD
    bi = lax.broadcasted_iota(jnp.int32, (2 * RWKV_HEAD, 2 * RWKV_HEAD), 0) < RWKV_HEAD
    bj = lax.broadcasted_iota(jnp.int32, (2 * RWKV_HEAD, 2 * RWKV_HEAD), 1) < RWKV_HEAD
    blockdiag = bi == bj
    eye = jnp.where(ri == ci, 1.0, 0.0)

    def headsum(x):
        s0 = jnp.sum(jnp.where(m0, x, 0.0), axis=-1, keepdims=True)
        s1 = jnp.sum(jnp.where(m0, 0.0, x), axis=-1, keepdims=True)
        return jnp.where(m0, s0, s1)

    for p in range(RWKV_PAIRS):
        sl = slice(p * 2 * RWKV_HEAD, (p + 1) * 2 * RWKV_HEAD)
        r_p, v_p, kmod_p, iclr_p = r[:, sl], v[:, sl], kmod[:, sl], iclr[:, sl]
        kkr = kkraw[:, sl]
        kk = kkr / jnp.maximum(jnp.sqrt(headsum(kkr * kkr)), 1e-12)
        a_p = -kk
        b_p = kk * iclr_p
        at = a_p * e_mprev[:, sl]
        rt = r_p * e_m[:, sl]
        bk = jnp.concatenate([b_p * e_mi[:, sl], kmod_p * e_mi[:, sl]], axis=0).astype(BF16)
        ar0 = jnp.concatenate([a_p * e_0prev[:, sl], r_p * e_0[:, sl]], axis=0).astype(BF16)
        be = jnp.concatenate([b_p * e_end[:, sl], kmod_p * e_end[:, sl]], axis=0).astype(BF16)
        wt = wt_s[p]
        a_s0 = _dot_nt(ar0, wt.astype(BF16))
        v_b = v_p.astype(BF16)

        u_h, m_h = [], []
        for h in range(2):
            mh = m0 if h == 0 else jnp.logical_not(m0)
            arm = jnp.concatenate([jnp.where(mh, at, 0.0), jnp.where(mh, rt, 0.0)], axis=0)
            gm = _dot_nt(arm.astype(BF16), bk)
            n_ab = jnp.where(strict, gm[0:L, 0:L], 0.0)
            n_ak = jnp.where(strict, gm[0:L, L:2 * L], 0.0)
            m_h.append(jnp.concatenate([jnp.where(incl, gm[L:2 * L, 0:L], 0.0),
                                        jnp.where(incl, gm[L:2 * L, L:2 * L], 0.0)], axis=1))
            rhs = a_s0[0:L] + _dot(n_ak.astype(BF16), v_b)
            x = eye + n_ab
            pw = n_ab
            for _ in range(int(math.log2(L)) - 1):
                pwb = pw.astype(BF16)
                pw = _dot(pwb, pwb)
                x = x + _dot(x.astype(BF16), pw.astype(BF16))
            u_h.append(_dot(x.astype(BF16), rhs.astype(BF16)))
        u_p = jnp.where(m0, u_h[0], u_h[1])
        uv = jnp.concatenate([u_p, v_p], axis=0).astype(BF16)
        o0 = a_s0[L:2 * L] + _dot(m_h[0].astype(BF16), uv)
        o1 = a_s0[L:2 * L] + _dot(m_h[1].astype(BF16), uv)
        o = jnp.where(m0, o0, o1)
        wt_new = wt * p_l[:, sl] + jnp.where(blockdiag, _dot_tn(uv, be), 0.0)
        wt_s[p] = wt_new
        wt_ref[p] = wt_new

        mean = headsum(o) * (1.0 / RWKV_HEAD)
        d = o - mean
        var = headsum(d * d) * (1.0 / RWKV_HEAD)
        on = d * lax.rsqrt(var + GN_EPS) * lnw_ref[:, sl] + lnb_ref[:, sl]
        bonus = headsum(r_p * kmod_p * rk_ref[:, sl]) * v_p
        y = (on + bonus) * g[:, sl]
        y_ref[:, sl] = y[0:Tv]


def _rwkv(zr, sh0, wt0, p, B, T):
    Tv = min(T, RWKV_CHUNK)
    kern = functools.partial(_rwkv_kernel, Tv=Tv)
    row = lambda n: pl.BlockSpec((1, n), lambda b, c: (0, 0))
    mat = lambda m, n: pl.BlockSpec((m, n), lambda b, c: (0, 0))
    st = pl.BlockSpec((None, RWKV_PAIRS, 128, 128), lambda b, c: (b, 0, 0, 0))
    return pl.pallas_call(
        kern,
        grid=(B, T // Tv),
        in_specs=[
            pl.BlockSpec((Tv, RWKV_COLS), lambda b, c: (c, b)),
            pl.BlockSpec((None, 1, RWKV_COLS), lambda b, c: (b, 0, 0)),
            st,
            row(RWKV_COLS), row(RWKV_WIDTH), mat(LORA_WA, RWKV_WIDTH), row(RWKV_WIDTH),
            mat(LORA_WA, RWKV_WIDTH), mat(G_LORA, RWKV_WIDTH),
            row(RWKV_WIDTH), row(RWKV_WIDTH), row(RWKV_WIDTH), row(RWKV_WIDTH), row(RWKV_WIDTH),
        ],
        out_specs=[
            pl.BlockSpec((Tv, RWKV_WIDTH), lambda b, c: (c, b)),
            st,
        ],
        out_shape=[
            jax.ShapeDtypeStruct((T, B * RWKV_WIDTH), F32),
            jax.ShapeDtypeStruct((B, RWKV_PAIRS, 128, 128), F32),
        ],
        scratch_shapes=[
            pltpu.VMEM((1, RWKV_COLS), F32),
            pltpu.VMEM((RWKV_PAIRS, 128, 128), F32),
        ],
        compiler_params=_cparams(("arbitrary", "arbitrary")),
        name="rwkv7",
    )(zr, sh0, wt0, p["mu"], p["w0"], p["w2"], p["a0"], p["a2"], p["g2"],
      p["kk"], p["ka"], p["rk"], p["lnw"], p["lnb"])


def _outproj_kernel(x_ref, yl_ref, yr_ref, wa_ref, wb_ref, o_ref):
    o_ref[...] = (x_ref[...] + _dot(yl_ref[...].astype(BF16), wa_ref[...])
                  + _dot(yr_ref[...].astype(BF16), wb_ref[...]))


def _outproj(x, yl, yr, wa, wb, tm):
    B, T, D = x.shape
    return pl.pallas_call(
        _outproj_kernel,
        grid=(B, T // tm),
        in_specs=[
            pl.BlockSpec((None, tm, D), lambda b, i: (b, i, 0)),
            pl.BlockSpec((tm, LRU_WIDTH), lambda b, i: (i, b)),
            pl.BlockSpec((tm, RWKV_WIDTH), lambda b, i: (i, b)),
            pl.BlockSpec((LRU_WIDTH, D), lambda b, i: (0, 0)),
            pl.BlockSpec((RWKV_WIDTH, D), lambda b, i: (0, 0)),
        ],
        out_specs=pl.BlockSpec((None, tm, D), lambda b, i: (b, i, 0)),
        out_shape=jax.ShapeDtypeStruct((B, T, D), F32),
        compiler_params=_cparams(("parallel", "parallel")),
        name="outproj",
    )(x, yl, yr, wa, wb)


def _top_rows(x, n):
    R = x.shape[0]
    ri = lax.broadcasted_iota(jnp.int32, x.shape, 0).astype(F32)
    rows = []
    for it in range(n):
        m = jnp.max(x, axis=0, keepdims=True)
        rows.append(m)
        if it + 1 < n:
            first = jnp.min(jnp.where(x == m, ri, float(R)), axis=0, keepdims=True)
            x = jnp.where(ri == first, -jnp.inf, x)
    return rows


def _peer_kernel(h_ref, g_ref, wq_ref, keys_ref, u_ref, vt_ref, o_ref,
                 xn_s, s1_s, s2_s, e1_s, e2_s, tau_s, g_s, acc_s, *, Tt, Ec):
    j = pl.program_id(1)
    nI = Ec // N_KEYS
    nT = Tt // 128
    K = PEER_TOPK

    @pl.when(j == 0)
    def _prep():
        xn = _rms(h_ref[...], g_ref[...]).astype(BF16)
        xn_s[...] = xn
        acc_s[...] = jnp.zeros_like(acc_s)
        for h in range(PEER_HEADS):
            q = _dot_nt(wq_ref[h * 2 * PEER_HALF:(h + 1) * 2 * PEER_HALF, :], xn)
            s1 = _dot(keys_ref[2 * h], q[0:PEER_HALF].astype(BF16))
            s2 = _dot(keys_ref[2 * h + 1], q[PEER_HALF:].astype(BF16))
            t1 = _top_rows(s1, K)
            t2 = _top_rows(s2, K)
            a1 = jnp.concatenate(t1, axis=0)
            a2 = jnp.concatenate(t2, axis=0)
            sub = lax.broadcasted_iota(jnp.int32, (8, Tt), 0)
            cand = [t1[0] + a2, t1[1] + a2[0:8]]
            for pp in range(2, 8):
                cand.append(jnp.where(sub < K // (pp + 1), t1[pp] + a2[0:8], -jnp.inf))
            cand.append(a1[8:16] + t2[0])
            best = _top_rows(jnp.concatenate(cand, axis=0), K)
            cmax = best[0]
            zsum = jnp.zeros_like(cmax)
            for b_ in best:
                zsum = zsum + jnp.exp(b_ - cmax)
            tau_s[h:h + 1, :] = best[K - 1]
            e1 = jnp.exp(s1 - t1[0]) / zsum
            e2 = jnp.exp(s2 - t2[0])
            for tb in range(nT):
                cs = slice(tb * 128, (tb + 1) * 128)
                s1_s[h, tb] = s1[:, cs]
                s2_s[h, tb] = s2[:, cs]
                e1_s[h, tb] = e1[:, cs]
                e2_s[h, tb] = e2[:, cs]

    for tb in range(nT):
        cs = slice(tb * 128, (tb + 1) * 128)
        taus = [tau_s[h:h + 1, cs] for h in range(PEER_HEADS)]

        def body(ii, carry, tb=tb, cs=cs, taus=taus):
            i1 = j * nI + ii
            acc = jnp.zeros((N_KEYS, 128), F32)
            for h in range(PEER_HEADS):
                cval = s1_s[h, tb, pl.ds(i1, 1), :] + s2_s[h, tb]
                gate = e1_s[h, tb, pl.ds(i1, 1), :] * e2_s[h, tb]
                acc = acc + jnp.where(cval >= taus[h], gate, 0.0)
            g_s[pl.ds(pl.multiple_of(ii * N_KEYS, N_KEYS), N_KEYS), cs] = acc
            return carry

        lax.fori_loop(0, nI, body, 0)

    hid = _dot_nt(u_ref[...], xn_s[...])
    wgt = (g_s[...] * _gelu(hid)).astype(BF16)
    acc_s[...] += _dot(vt_ref[...], wgt)

    @pl.when(j == pl.num_programs(1) - 1)
    def _fin():
        o_ref[...] = h_ref[...] + acc_s[...].T


def _peer(h, g, wq_t, keys, u, vt, Tt, Ec):
    N, D = h.shape
    nT = Tt // 128
    kern = functools.partial(_peer_kernel, Tt=Tt, Ec=Ec)
    score = pltpu.VMEM((PEER_HEADS, nT, N_KEYS, 128), F32)
    return pl.pallas_call(
        kern,
        grid=(N // Tt, N_EXPERTS // Ec),
        in_specs=[
            pl.BlockSpec((Tt, D), lambda i, j: (i, 0)),
            pl.BlockSpec((1, D), lambda i, j: (0, 0)),
            pl.BlockSpec((PEER_HEADS * 2 * PEER_HALF, D), lambda i, j: (0, 0)),
            pl.BlockSpec((2 * PEER_HEADS, N_KEYS, PEER_HALF), lambda i, j: (0, 0, 0)),
            pl.BlockSpec((Ec, D), lambda i, j: (j, 0)),
            pl.BlockSpec((D, Ec), lambda i, j: (0, j)),
        ],
        out_specs=pl.BlockSpec((Tt, D), lambda i, j: (i, 0)),
        out_shape=jax.ShapeDtypeStruct((N, D), F32),
        scratch_shapes=[
            pltpu.VMEM((Tt, D), BF16),
            score, score, score, score,
            pltpu.VMEM((PEER_HEADS, Tt), F32),
            pltpu.VMEM((Ec, Tt), F32),
            pltpu.VMEM((D, Tt), F32),
        ],
        compiler_params=_cparams(("parallel", "arbitrary")),
        name="peer",
    )(h, g, wq_t, keys, u, vt)


def _ple_kernel(h_ref, p_ref, g_ref, pw_ref, gw_ref, fg_ref, o_ref, *, final):
    h = h_ref[...]
    gate = _sigmoid(_dot(_rms(h, g_ref[...]).astype(BF16), gw_ref[...]))
    out = h + _dot(p_ref[...].astype(BF16), pw_ref[...]) * gate
    if final:
        out = _rms(out, fg_ref[...])
    o_ref[...] = out


def _ple(h, p, g, pw, gw, fg, tm, final):
    N, D = h.shape
    return pl.pallas_call(
        functools.partial(_ple_kernel, final=final),
        grid=(N // tm,),
        in_specs=[
            pl.BlockSpec((tm, D), lambda i: (i, 0)),
            pl.BlockSpec((tm, D_PLE), lambda i: (i, 0)),
            pl.BlockSpec((1, D), lambda i: (0, 0)),
            pl.BlockSpec((D_PLE, D), lambda i: (0, 0)),
            pl.BlockSpec((D, D), lambda i: (0, 0)),
            pl.BlockSpec((1, D), lambda i: (0, 0)),
        ],
        out_specs=pl.BlockSpec((tm, D), lambda i: (i, 0)),
        out_shape=jax.ShapeDtypeStruct((N, D), F32),
        compiler_params=_cparams(("parallel",)),
        name="ple",
    )(h, p, g, pw, gw, fg)


def _layer_params(i, norm1_g, w_in, conv_w, conv_b, lru_wa, lru_ba, lru_wx, lru_bx, lru_lambda,
                  rwkv_mu, rwkv_w0, rwkv_w2, rwkv_a0, rwkv_a2, rwkv_g2, rwkv_kk, rwkv_ka, rwkv_rk,
                  rwkv_ln_w, rwkv_ln_b, w_out, norm2_g, peer_wq, peer_keys, peer_u, peer_v,
                  norm3_g, ple_w, ple_gate):
    row = lambda a: a.reshape(1, -1)
    wi = w_in[i].astype(BF16)
    zpad = jnp.zeros((LORA_WA // 2, RWKV_WIDTH), BF16)
    return dict(
        g1=row(norm1_g[i]), wl=wi[:, :LRU_COLS], wr=wi[:, LRU_COLS:],
        lru=dict(cw=conv_w[i], cb=row(conv_b[i]), wa=lru_wa[i].astype(BF16), ba=row(lru_ba[i]),
                 wx=lru_wx[i].astype(BF16), bx=row(lru_bx[i]), lam=row(lru_lambda[i])),
        rwkv=dict(mu=row(rwkv_mu[i]), w0=row(rwkv_w0[i]),
                  w2=jnp.concatenate([rwkv_w2[i].astype(BF16), zpad], axis=0), a0=row(rwkv_a0[i]),
                  a2=jnp.concatenate([zpad, rwkv_a2[i].astype(BF16)], axis=0),
                  g2=rwkv_g2[i].astype(BF16), kk=row(rwkv_kk[i]), ka=row(rwkv_ka[i]),
                  rk=row(rwkv_rk[i]), lnw=row(rwkv_ln_w[i]), lnb=row(rwkv_ln_b[i])),
        wo_a=w_out[i, :LRU_WIDTH].astype(BF16), wo_b=w_out[i, LRU_WIDTH:].astype(BF16),
        g2=row(norm2_g[i]), wq_t=peer_wq[i].T.astype(BF16),
        keys=peer_keys[i].reshape(2 * PEER_HEADS, N_KEYS, PEER_HALF).astype(BF16),
        u=peer_u[i].astype(BF16), vt=peer_v[i].T.astype(BF16),
        g3=row(norm3_g[i]), pw=ple_w[i].astype(BF16), gw=ple_gate[i].astype(BF16),
    )


def _pairs_from_state(s):
    B = s.shape[0]
    s = s.reshape(B, RWKV_PAIRS, 2, RWKV_HEAD, RWKV_HEAD)
    z = jnp.zeros_like(s[:, :, 0])
    top = jnp.concatenate([s[:, :, 0], z], axis=-1)
    bot = jnp.concatenate([z, s[:, :, 1]], axis=-1)
    return jnp.concatenate([top, bot], axis=-2)


def _state_from_pairs(w):
    B = w.shape[0]
    w = w.reshape(B, RWKV_PAIRS, 2, RWKV_HEAD, 2, RWKV_HEAD)
    return jnp.stack([w[:, :, 0, :, 0, :], w[:, :, 1, :, 1, :]], axis=2).reshape(
        B, RWKV_HEADS, RWKV_HEAD, RWKV_HEAD)


def _trunk(x, p, conv_st, lru_st, shift_st, wkv_st, first, layers, final_g, tiles):
    B, T, D = x.shape
    depth = len(layers)
    tm, lt, tt, ec = tiles
    h = x
    convs, lrus, shifts, wkvs = [], [], [], []
    for i, lp in enumerate(layers):
        zl, zr = _inproj(h, lp["g1"], lp["wl"], lp["wr"], tm)
        conv0 = jnp.swapaxes(conv_st[i], 0, 1).reshape((CONV_W - 1) * B, LRU_WIDTH)
        y_lru, h_lru = _lru(zl.reshape(T * B, LRU_COLS), conv0, lru_st[i], lp["lru"], B, T, lt, first)
        y_rw, wt = _rwkv(zr, shift_st[i].reshape(B, 1, RWKV_COLS), _pairs_from_state(wkv_st[i]),
                         lp["rwkv"], B, T)
        h1 = _outproj(h, y_lru.reshape(T, B * LRU_WIDTH), y_rw, lp["wo_a"], lp["wo_b"], tm)
        h2 = _peer(h1.reshape(B * T, D), lp["g2"], lp["wq_t"], lp["keys"], lp["u"], lp["vt"], tt, ec)
        h3 = _ple(h2, p[i].reshape(B * T, D_PLE), lp["g3"], lp["pw"], lp["gw"], final_g, tt,
                  final=(i == depth - 1))
        h = h3.reshape(B, T, D)
        zl3 = zl.reshape(T, B, LRU_COLS)
        convs.append(jnp.swapaxes(zl3[T - (CONV_W - 1):, :, :LRU_WIDTH], 0, 1))
        lrus.append(h_lru)
        shifts.append(zr.reshape(T, B, RWKV_COLS)[T - 1])
        wkvs.append(_state_from_pairs(wt))
    return h, jnp.stack(convs), jnp.stack(lrus), jnp.stack(shifts), jnp.stack(wkvs)


def kernel(x_prompt, x_sample, p_prompt, p_sample, state_conv, state_lru, state_shift, state_wkv, norm1_g, w_in, conv_w, conv_b, lru_wa, lru_ba, lru_wx, lru_bx, lru_lambda, rwkv_mu, rwkv_w0, rwkv_w2, rwkv_a0, rwkv_a2, rwkv_g2, rwkv_kk, rwkv_ka, rwkv_rk, rwkv_ln_w, rwkv_ln_b, w_out, norm2_g, peer_wq, peer_keys, peer_u, peer_v, norm3_g, ple_w, ple_gate, final_g):
    depth = w_in.shape[0]
    layers = [_layer_params(i, norm1_g, w_in, conv_w, conv_b, lru_wa, lru_ba, lru_wx, lru_bx,
                            lru_lambda, rwkv_mu, rwkv_w0, rwkv_w2, rwkv_a0, rwkv_a2, rwkv_g2,
                            rwkv_kk, rwkv_ka, rwkv_rk, rwkv_ln_w, rwkv_ln_b, w_out, norm2_g,
                            peer_wq, peer_keys, peer_u, peer_v, norm3_g, ple_w, ple_gate)
              for i in range(depth)]
    fg = final_g.reshape(1, -1)
    bp = x_prompt.shape[0]
    dt = x_prompt.dtype
    zc = jnp.zeros((depth, bp, CONV_W - 1, LRU_WIDTH), dt)
    zl = jnp.zeros((depth, bp, LRU_WIDTH), dt)
    zs = jnp.zeros((depth, bp, RWKV_COLS), dt)
    zw = jnp.zeros((depth, bp, RWKV_HEADS, RWKV_HEAD, RWKV_HEAD), dt)
    tp = min(512, x_prompt.shape[1])
    ts = min(512, x_sample.shape[1])
    out_p = _trunk(x_prompt, p_prompt, zc, zl, zs, zw, True, layers, fg,
                   (tp, min(32, x_prompt.shape[1]), 512, 512))
    out_s = _trunk(x_sample, p_sample, state_conv, state_lru, state_shift, state_wkv, False,
                   layers, fg, (ts, min(32, x_sample.shape[1]), 512, 512))
    return (out_p[0], out_s[0]) + tuple(out_p[1:]) + tuple(out_s[1:])
```

```python
import functools
import math

import jax
import jax.numpy as jnp
from jax import lax
from jax.experimental import pallas as pl
from jax.experimental.pallas import tpu as pltpu

F32 = jnp.float32
BF16 = jnp.bfloat16

D_MODEL = 1024
D_PLE = 256
LRU_WIDTH = 512
LRU_BLOCKS = 4
LRU_BLOCK = 128
CONV_W = 4
LRU_C = 8.0
RWKV_WIDTH = 512
RWKV_HEAD = 64
RWKV_HEADS = 8
RWKV_PAIRS = RWKV_HEADS // 2
LORA_WA = 128
G_LORA = 128
RWKV_COLS = 3 * RWKV_WIDTH + LORA_WA + G_LORA
LRU_COLS = 2 * LRU_WIDTH
PEER_HEADS = 8
N_KEYS = 128
N_EXPERTS = N_KEYS * N_KEYS
PEER_HALF = 128
PEER_TOPK = 16
EPS = 1e-6
GN_EPS = 64e-5

RWKV_CHUNK = 128
VMEM_LIMIT = 56 * 1024 * 1024


def _cparams(sem):
    return pltpu.CompilerParams(dimension_semantics=sem, vmem_limit_bytes=VMEM_LIMIT)


def _rms(x, g):
    return x * lax.rsqrt(jnp.mean(x * x, axis=-1, keepdims=True) + EPS) * g


def _gelu(x):
    c = math.sqrt(2.0 / math.pi)
    return x * (0.5 * (1.0 + jnp.tanh(c * (x + 0.044715 * (x * x * x)))))


def _sigmoid(x):
    return 1.0 / (1.0 + jnp.exp(-x))


def _softplus(x):
    return jnp.maximum(x, 0.0) + jnp.log1p(jnp.exp(-jnp.abs(x)))


def _dot(a, b):
    return jnp.dot(a, b, preferred_element_type=F32)


def _dot_nt(a, b):
    return lax.dot_general(a, b, (((1,), (1,)), ((), ())), preferred_element_type=F32)


def _dot_tn(a, b):
    return lax.dot_general(a, b, (((0,), (0,)), ((), ())), preferred_element_type=F32)


def _inproj_kernel(x_ref, g_ref, wl_ref, wr_ref, zl_ref, zr_ref):
    xn = _rms(x_ref[...], g_ref[...]).astype(BF16)
    zl_ref[...] = _dot(xn, wl_ref[...])
    zr_ref[...] = _dot(xn, wr_ref[...])


def _inproj(x, g, wl, wr, tm):
    B, T, D = x.shape
    return pl.pallas_call(
        _inproj_kernel,
        grid=(B, T // tm),
        in_specs=[
            pl.BlockSpec((None, tm, D), lambda b, i: (b, i, 0)),
            pl.BlockSpec((1, D), lambda b, i: (0, 0)),
            pl.BlockSpec((D, LRU_COLS), lambda b, i: (0, 0)),
            pl.BlockSpec((D, RWKV_COLS), lambda b, i: (0, 0)),
        ],
        out_specs=[
            pl.BlockSpec((tm, LRU_COLS), lambda b, i: (i, b)),
            pl.BlockSpec((tm, RWKV_COLS), lambda b, i: (i, b)),
        ],
        out_shape=[
            jax.ShapeDtypeStruct((T, B * LRU_COLS), F32),
            jax.ShapeDtypeStruct((T, B * RWKV_COLS), F32),
        ],
        compiler_params=_cparams(("parallel", "parallel")),
        name="inproj",
    )(x, g, wl, wr)


def _lru_kernel(z_ref, conv0_ref, h0_ref, cw_ref, cb_ref, wa_ref, ba_ref, wx_ref, bx_ref,
                lam_ref, y_ref, hlast_ref, hist_s, h_s, a_s, b_s, *, B, Lt, first):
    i = pl.program_id(0)
    W = LRU_WIDTH
    R = Lt * B

    @pl.when(i == 0)
    def _():
        hist_s[...] = conv0_ref[...]
        h_s[...] = h0_ref[...]

    xb = z_ref[:, 0:W]
    gb = z_ref[:, W:2 * W]
    full = jnp.concatenate([hist_s[...], xb], axis=0)
    xc = cb_ref[...] + full[0:R] * cw_ref[0:1, :]
    for j in range(1, CONV_W):
        xc = xc + full[j * B:j * B + R] * cw_ref[j:j + 1, :]
    hist_s[...] = full[R:R + (CONV_W - 1) * B]

    ra, ix = [], []
    for n in range(LRU_BLOCKS):
        xn = xc[:, n * LRU_BLOCK:(n + 1) * LRU_BLOCK].astype(BF16)
        ra.append(_dot(xn, wa_ref[n]))
        ix.append(_dot(xn, wx_ref[n]))
    r = _sigmoid(jnp.concatenate(ra, axis=1) + ba_ref[...])
    ig = _sigmoid(jnp.concatenate(ix, axis=1) + bx_ref[...])
    log_a = -LRU_C * r * _softplus(-lam_ref[...])
    a = jnp.exp(log_a)
    t = jnp.tanh(log_a)
    mult = jnp.sqrt(-2.0 * t / (1.0 - t))
    if first:
        row = lax.broadcasted_iota(jnp.int32, (R, 1), 0)
        mult = jnp.where(row < jnp.where(i == 0, B, 0), 1.0, mult)
    a_s[...] = a
    b_s[...] = mult * (ig * xc)

    def step(tt, h):
        r0 = pl.multiple_of(tt * B, B)
        h = a_s[pl.ds(r0, B), :] * h + b_s[pl.ds(r0, B), :]
        b_s[pl.ds(r0, B), :] = h
        return h

    h = lax.fori_loop(0, Lt, step, h_s[...], unroll=8)
    h_s[...] = h
    hlast_ref[...] = h
    y_ref[...] = b_s[...] * _gelu(gb)


def _lru(zl, conv0, h0, p, B, T, Lt, first):
    W = LRU_WIDTH
    R = Lt * B
    kern = functools.partial(_lru_kernel, B=B, Lt=Lt, first=first)
    full = lambda shape: pl.BlockSpec(shape, lambda i: (0,) * len(shape))
    return pl.pallas_call(
        kern,
        grid=(T // Lt,),
        in_specs=[
            pl.BlockSpec((R, LRU_COLS), lambda i: (i, 0)),
            full(((CONV_W - 1) * B, W)),
            full((B, W)),
            full((CONV_W, W)),
            full((1, W)),
            full((LRU_BLOCKS, LRU_BLOCK, LRU_BLOCK)),
            full((1, W)),
            full((LRU_BLOCKS, LRU_BLOCK, LRU_BLOCK)),
            full((1, W)),
            full((1, W)),
        ],
        out_specs=[
            pl.BlockSpec((R, W), lambda i: (i, 0)),
            full((B, W)),
        ],
        out_shape=[
            jax.ShapeDtypeStruct((T * B, W), F32),
            jax.ShapeDtypeStruct((B, W), F32),
        ],
        scratch_shapes=[
            pltpu.VMEM(((CONV_W - 1) * B, W), F32),
            pltpu.VMEM((B, W), F32),
            pltpu.VMEM((R, W), F32),
            pltpu.VMEM((R, W), F32),
        ],
        compiler_params=_cparams(("arbitrary",)),
        name="rglru",
    )(zl, conv0, h0, p["cw"], p["cb"], p["wa"], p["ba"], p["wx"], p["bx"], p["lam"])


def _rwkv_kernel(z_ref, sh0_ref, wt0_ref, mu_ref, w0_ref, w2_ref, a0_ref, a2_ref, g2_ref,
                 kk_ref, ka_ref, rk_ref, lnw_ref, lnb_ref, y_ref, wt_ref, zprev_s, wt_s, *, Tv):
    L = RWKV_CHUNK
    Wd = RWKV_WIDTH
    c = pl.program_id(1)

    @pl.when(c == 0)
    def _():
        zprev_s[...] = sh0_ref[...]
        wt_s[...] = wt0_ref[...]

    z = z_ref[...]
    if Tv < L:
        z = jnp.concatenate([z, jnp.zeros((L - Tv, RWKV_COLS), F32)], axis=0)
    row = lax.broadcasted_iota(jnp.int32, (L, 1), 0)
    zp = jnp.where(row == 0, zprev_s[...], pltpu.roll(z, 1, 0))
    zprev_s[...] = z[Tv - 1:Tv, :]
    zs = z + (zp - z) * mu_ref[...]
    r = zs[:, 0:Wd]
    k = zs[:, Wd:2 * Wd]
    v = zs[:, 2 * Wd:3 * Wd]
    zwa = zs[:, 3 * Wd:3 * Wd + LORA_WA]
    zg = zs[:, 3 * Wd + LORA_WA:]
    if Tv < L:
        valid = row < Tv
        r = jnp.where(valid, r, 0.0)
        k = jnp.where(valid, k, 0.0)
        v = jnp.where(valid, v, 0.0)
    w_log = -_softplus(-(w0_ref[...] + _dot(jnp.tanh(zwa).astype(BF16), w2_ref[...]))) - 0.5
    ld = -jnp.exp(w_log)
    if Tv < L:
        ld = jnp.where(valid, ld, 0.0)
    iclr = _sigmoid(a0_ref[...] + _dot(zwa.astype(BF16), a2_ref[...]))
    g = _dot(_sigmoid(zg).astype(BF16), g2_ref[...])
    kkraw = k * kk_ref[...]
    kmod = k * (1.0 + (iclr - 1.0) * ka_ref[...])

    ri = lax.broadcasted_iota(jnp.int32, (L, L), 0)
    ci = lax.broadcasted_iota(jnp.int32, (L, L), 1)
    strict = ri > ci
    incl = ri >= ci
    tri = jnp.where(incl, 1.0, 0.0).astype(BF16)
    ld_hi = ld.astype(BF16)
    rem = ld - ld_hi.astype(F32)
    ld_mid = rem.astype(BF16)
    ld_lo = (rem - ld_mid.astype(F32)).astype(BF16)
    cum = _dot(tri, ld_hi) + _dot(tri, ld_mid) + _dot(tri, ld_lo)
    cum_l = cum[L - 1:L, :]
    cumm = cum - cum[L // 2 - 1:L // 2, :]
    e_m = jnp.exp(cumm)
    e_mi = jnp.exp(-cumm)
    e_mprev = jnp.exp(cumm - ld)
    e_0 = jnp.exp(cum)
    e_0prev = jnp.exp(cum - ld)
    e_end = jnp.exp(cum_l - cum)
    p_l = jnp.exp(cum_l)
    lane = lax.broadcasted_iota(jnp.int32, (1, 2 * RWKV_HEAD), 1)
    m0 = lane < RWKV_HEAD
    bi = lax.broadcasted_iota(jnp.int32, (2 * RWKV_HEAD, 2 * RWKV_HEAD), 0) < RWKV_HEAD
    bj = lax.broadcasted_iota(jnp.int32, (2 * RWKV_HEAD, 2 * RWKV_HEAD), 1) < RWKV_HEAD
    blockdiag = bi == bj

    def headsum(x):
        s0 = jnp.sum(jnp.where(m0, x, 0.0), axis=-1, keepdims=True)
        s1 = jnp.sum(jnp.where(m0, 0.0, x), axis=-1, keepdims=True)
        return jnp.where(m0, s0, s1)

    P = range(RWKV_PAIRS)
    sls = [slice(p * 128, (p + 1) * 128) for p in P]
    pre = []
    for p in P:
        sl = sls[p]
        r_p, v_p, kmod_p, iclr_p = r[:, sl], v[:, sl], kmod[:, sl], iclr[:, sl]
        kkr = kkraw[:, sl]
        kk = kkr / jnp.maximum(jnp.sqrt(headsum(kkr * kkr)), 1e-12)
        a_p = -kk
        b_p = kk * iclr_p
        at = a_p * e_mprev[:, sl]
        rt = r_p * e_m[:, sl]
        bk = jnp.concatenate([b_p * e_mi[:, sl], kmod_p * e_mi[:, sl]], axis=0).astype(BF16)
        ar0 = jnp.concatenate([a_p * e_0prev[:, sl], r_p * e_0[:, sl]], axis=0).astype(BF16)
        be = jnp.concatenate([b_p * e_end[:, sl], kmod_p * e_end[:, sl]], axis=0).astype(BF16)
        pre.append(dict(r=r_p, v=v_p, kmod=kmod_p, at=at, rt=rt, bk=bk, ar0=ar0, be=be,
                        vb=v_p.astype(BF16)))
    wts = [wt_s[p] for p in P]
    as0 = [_dot_nt(pre[p]["ar0"], wts[p].astype(BF16)) for p in P]
    heads = [(p, h) for p in P for h in range(2)]
    gms = {}
    for (p, h) in heads:
        mh = m0 if h == 0 else jnp.logical_not(m0)
        arm = jnp.concatenate([jnp.where(mh, pre[p]["at"], 0.0), jnp.where(mh, pre[p]["rt"], 0.0)],
                              axis=0)
        gms[(p, h)] = _dot_nt(arm.astype(BF16), pre[p]["bk"])
    nab, mm, ys = {}, {}, {}
    for (p, h) in heads:
        gm = gms[(p, h)]
        nab[(p, h)] = jnp.where(strict, gm[0:L, 0:L], 0.0).astype(BF16)
        n_ak = jnp.where(strict, gm[0:L, L:2 * L], 0.0)
        mm[(p, h)] = jnp.concatenate([jnp.where(incl, gm[L:2 * L, 0:L], 0.0),
                                      jnp.where(incl, gm[L:2 * L, L:2 * L], 0.0)],
                                     axis=1).astype(BF16)
        ys[(p, h)] = as0[p][0:L] + _dot(n_ak.astype(BF16), pre[p]["vb"])
    pws = dict(nab)
    nlev = int(math.log2(L))
    for lev in range(nlev):
        for hd in heads:
            ys[hd] = ys[hd] + _dot(pws[hd], ys[hd].astype(BF16))
        if lev + 1 < nlev:
            for hd in heads:
                pws[hd] = _dot(pws[hd], pws[hd]).astype(BF16)
    for p in P:
        sl = sls[p]
        pp = pre[p]
        u_p = jnp.where(m0, ys[(p, 0)], ys[(p, 1)])
        uv = jnp.concatenate([u_p, pp["v"]], axis=0).astype(BF16)
        o0 = as0[p][L:2 * L] + _dot(mm[(p, 0)], uv)
        o1 = as0[p][L:2 * L] + _dot(mm[(p, 1)], uv)
        o = jnp.where(m0, o0, o1)
        wt_new = wts[p] * p_l[:, sl] + jnp.where(blockdiag, _dot_tn(uv, pp["be"]), 0.0)
        wt_s[p] = wt_new
        wt_ref[p] = wt_new
        mean = headsum(o) * (1.0 / RWKV_HEAD)
        d = o - mean
        var = headsum(d * d) * (1.0 / RWKV_HEAD)
        on = d * lax.rsqrt(var + GN_EPS) * lnw_ref[:, sl] + lnb_ref[:, sl]
        bonus = headsum(pp["r"] * pp["kmod"] * rk_ref[:, sl]) * pp["v"]
        y = (on + bonus) * g[:, sl]
        y_ref[:, sl] = y[0:Tv]


def _rwkv(zr, sh0, wt0, p, B, T):
    Tv = min(T, RWKV_CHUNK)
    kern = functools.partial(_rwkv_kernel, Tv=Tv)
    row = lambda n: pl.BlockSpec((1, n), lambda b, c: (0, 0))
    mat = lambda m, n: pl.BlockSpec((m, n), lambda b, c: (0, 0))
    st = pl.BlockSpec((None, RWKV_PAIRS, 128, 128), lambda b, c: (b, 0, 0, 0))
    return pl.pallas_call(
        kern,
        grid=(B, T // Tv),
        in_specs=[
            pl.BlockSpec((Tv, RWKV_COLS), lambda b, c: (c, b)),
            pl.BlockSpec((None, 1, RWKV_COLS), lambda b, c: (b, 0, 0)),
            st,
            row(RWKV_COLS), row(RWKV_WIDTH), mat(LORA_WA, RWKV_WIDTH), row(RWKV_WIDTH),
            mat(LORA_WA, RWKV_WIDTH), mat(G_LORA, RWKV_WIDTH),
            row(RWKV_WIDTH), row(RWKV_WIDTH), row(RWKV_WIDTH), row(RWKV_WIDTH), row(RWKV_WIDTH),
        ],
        out_specs=[
            pl.BlockSpec((Tv, RWKV_WIDTH), lambda b, c: (c, b)),
            st,
        ],
        out_shape=[
            jax.ShapeDtypeStruct((T, B * RWKV_WIDTH), F32),
            jax.ShapeDtypeStruct((B, RWKV_PAIRS, 128, 128), F32),
        ],
        scratch_shapes=[
            pltpu.VMEM((1, RWKV_COLS), F32),
            pltpu.VMEM((RWKV_PAIRS, 128, 128), F32),
        ],
        compiler_params=_cparams(("arbitrary", "arbitrary")),
        name="rwkv7",
    )(zr, sh0, wt0, p["mu"], p["w0"], p["w2"], p["a0"], p["a2"], p["g2"],
      p["kk"], p["ka"], p["rk"], p["lnw"], p["lnb"])


def _outproj_kernel(x_ref, yl_ref, yr_ref, wa_ref, wb_ref, o_ref):
    o_ref[...] = (x_ref[...] + _dot(yl_ref[...].astype(BF16), wa_ref[...])
                  + _dot(yr_ref[...].astype(BF16), wb_ref[...]))


def _outproj(x, yl, yr, wa, wb, tm):
    B, T, D = x.shape
    return pl.pallas_call(
        _outproj_kernel,
        grid=(B, T // tm),
        in_specs=[
            pl.BlockSpec((None, tm, D), lambda b, i: (b, i, 0)),
            pl.BlockSpec((tm, LRU_WIDTH), lambda b, i: (i, b)),
            pl.BlockSpec((tm, RWKV_WIDTH), lambda b, i: (i, b)),
            pl.BlockSpec((LRU_WIDTH, D), lambda b, i: (0, 0)),
            pl.BlockSpec((RWKV_WIDTH, D), lambda b, i: (0, 0)),
        ],
        out_specs=pl.BlockSpec((None, tm, D), lambda b, i: (b, i, 0)),
        out_shape=jax.ShapeDtypeStruct((B, T, D), F32),
        compiler_params=_cparams(("parallel", "parallel")),
        name="outproj",
    )(x, yl, yr, wa, wb)


def _sort16_pairs():
    pairs = []

    def merge(lo, n, r):
        step = 2 * r
        if step < n:
            merge(lo, n, step)
            merge(lo + r, n, step)
            for i in range(lo + r, lo + n - r, step):
                pairs.append((i, i + r))
        else:
            pairs.append((lo, lo + r))

    def sort(lo, n):
        if n > 1:
            sort(lo, n // 2)
            sort(lo + n // 2, n // 2)
            merge(lo, n, 1)

    sort(0, 16)
    return pairs


_SORT16 = _sort16_pairs()


def _merge_top(g, n):
    g = list(g)
    S = g[0].shape[0]
    sub = lax.broadcasted_iota(jnp.int32, g[0].shape, 0).astype(F32)
    rows = []
    for it in range(n):
        m = jnp.max(g[0], axis=0, keepdims=True)
        rows.append(m)
        if it + 1 < n:
            first = jnp.min(jnp.where(g[0] == m, sub, float(S)), axis=0, keepdims=True)
            sel = sub == first
            for v in range(min(n - 1 - it, len(g) - 1)):
                g[v] = jnp.where(sel, g[v + 1], g[v])
            if len(g) <= n - 1 - it:
                g[-1] = jnp.where(sel, -jnp.inf, g[-1])
    return rows


def _top16_rows(x):
    g = [x[8 * v:8 * v + 8, :] for v in range(16)]
    for (i, j) in _SORT16:
        g[i], g[j] = jnp.maximum(g[i], g[j]), jnp.minimum(g[i], g[j])
    return _merge_top(g, PEER_TOPK)


def _gelu2(x):
    c = math.sqrt(2.0 / math.pi)
    inner = x * (c + (c * 0.044715) * (x * x))
    hx = 0.5 * x
    return hx + hx * jnp.tanh(inner)


def _peer_kernel(h_ref, g_ref, wq_ref, keys_ref, u_ref, vt_ref, o_ref,
                 xnt_s, s1_s, s2_s, tau_s, s1x_s, w_s, hid_s, acc_s, *, Tt, Ec):
    j = pl.program_id(1)
    nI = Ec // N_KEYS
    nT = Tt // 128
    Kk = PEER_TOPK
    log2e = 1.0 / math.log(2.0)

    @pl.when(j == 0)
    def _prep():
        xn = _rms(h_ref[...], g_ref[...])
        xnb = xn.astype(BF16)
        xnt_s[...] = xn.T.astype(BF16)
        acc_s[...] = jnp.zeros_like(acc_s)
        sub = lax.broadcasted_iota(jnp.int32, (Kk, Tt), 0)

        def cands(a1, t2):
            return [jnp.where(sub < Kk // (q + 1), a1 + t2[q], -jnp.inf) for q in range(Kk)]

        for h in range(PEER_HEADS):
            q = _dot_nt(wq_ref[h * 2 * PEER_HALF:(h + 1) * 2 * PEER_HALF, :], xnb)
            s1 = _dot(keys_ref[2 * h], q[0:PEER_HALF].astype(BF16))
            s2 = _dot(keys_ref[2 * h + 1], q[PEER_HALF:].astype(BF16))
            t1 = _top16_rows(s1)
            t2 = _top16_rows(s2)
            sc1 = (s1 - t1[0]) * log2e
            sc2 = (s2 - t2[0]) * log2e
            a1 = (jnp.concatenate(t1, axis=0) - t1[0]) * log2e
            a2 = [(t - t2[0]) * log2e for t in t2]
            zsum = jnp.zeros_like(t1[0])
            for b_ in _merge_top(cands(a1, a2), Kk):
                zsum = zsum + jnp.exp2(b_)
            lz = jnp.log2(zsum)
            sc2 = sc2 - lz
            a2 = [t - lz for t in a2]
            tau_s[h:h + 1, :] = _merge_top(cands(a1, a2), Kk)[Kk - 1]
            for tb in range(nT):
                cs = slice(tb * 128, (tb + 1) * 128)
                s1_s[h, tb] = sc1[:, cs]
                s2_s[h, tb] = sc2[:, cs]

    HK = N_KEYS // 2
    IG = 2
    r0d = pl.multiple_of(j * nI, nI)
    for h in range(PEER_HEADS):
        for tb in range(nT):
            s1x_s[h, tb] = s1_s[h, tb, pl.ds(r0d, nI), :]
    hid_s[...] = _dot(u_ref[...], xnt_s[...])
    for tb in range(nT):
        cs = slice(tb * 128, (tb + 1) * 128)
        for half in range(2):
            ks = slice(half * HK, (half + 1) * HK)
            for ig in range(0, nI, IG):
                accs = [jnp.zeros((HK, 128), F32) for _ in range(IG)]
                for h in range(PEER_HEADS):
                    s2h = s2_s[h, tb, ks, :]
                    tau = tau_s[h:h + 1, cs]
                    for ii in range(IG):
                        i1 = ig + ii
                        cval = s1x_s[h, tb, i1:i1 + 1, :] + s2h
                        accs[ii] = accs[ii] + jnp.where(cval >= tau, jnp.exp2(cval), 0.0)
                for ii in range(IG):
                    r0 = (ig + ii) * N_KEYS + half * HK
                    rs = slice(r0, r0 + HK)
                    w_s[rs, cs] = (accs[ii] * _gelu2(hid_s[rs, cs])).astype(BF16)
    acc_s[...] += _dot(vt_ref[...], w_s[...])

    @pl.when(j == pl.num_programs(1) - 1)
    def _fin():
        o_ref[...] = h_ref[...] + acc_s[...].T


def _peer(h, g, wq_t, keys, u, vt, Tt, Ec):
    N, D = h.shape
    nT = Tt // 128
    nI = Ec // N_KEYS
    kern = functools.partial(_peer_kernel, Tt=Tt, Ec=Ec)
    score = pltpu.VMEM((PEER_HEADS, nT, N_KEYS, 128), F32)
    rows = pltpu.VMEM((PEER_HEADS, nT, nI, 128), F32)
    return pl.pallas_call(
        kern,
        grid=(N // Tt, N_EXPERTS // Ec),
        in_specs=[
            pl.BlockSpec((Tt, D), lambda i, j: (i, 0)),
            pl.BlockSpec((1, D), lambda i, j: (0, 0)),
            pl.BlockSpec((PEER_HEADS * 2 * PEER_HALF, D), lambda i, j: (0, 0)),
            pl.BlockSpec((2 * PEER_HEADS, N_KEYS, PEER_HALF), lambda i, j: (0, 0, 0)),
            pl.BlockSpec((Ec, D), lambda i, j: (j, 0)),
            pl.BlockSpec((D, Ec), lambda i, j: (0, j)),
        ],
        out_specs=pl.BlockSpec((Tt, D), lambda i, j: (i, 0)),
        out_shape=jax.ShapeDtypeStruct((N, D), F32),
        scratch_shapes=[
            pltpu.VMEM((D, Tt), BF16),
            score, score,
            pltpu.VMEM((PEER_HEADS, Tt), F32),
            rows,
            pltpu.VMEM((Ec, Tt), BF16),
            pltpu.VMEM((Ec, Tt), F32),
            pltpu.VMEM((D, Tt), F32),
        ],
        compiler_params=_cparams(("parallel", "arbitrary")),
        name="peer",
    )(h, g, wq_t, keys, u, vt)


def _ple_kernel(h_ref, p_ref, g_ref, pw_ref, gw_ref, fg_ref, o_ref, *, final):
    h = h_ref[...]
    gate = _sigmoid(_dot(_rms(h, g_ref[...]).astype(BF16), gw_ref[...]))
    out = h + _dot(p_ref[...].astype(BF16), pw_ref[...]) * gate
    if final:
        out = _rms(out, fg_ref[...])
    o_ref[...] = out


def _ple(h, p, g, pw, gw, fg, tm, final):
    N, D = h.shape
    return pl.pallas_call(
        functools.partial(_ple_kernel, final=final),
        grid=(N // tm,),
        in_specs=[
            pl.BlockSpec((tm, D), lambda i: (i, 0)),
            pl.BlockSpec((tm, D_PLE), lambda i: (i, 0)),
            pl.BlockSpec((1, D), lambda i: (0, 0)),
            pl.BlockSpec((D_PLE, D), lambda i: (0, 0)),
            pl.BlockSpec((D, D), lambda i: (0, 0)),
            pl.BlockSpec((1, D), lambda i: (0, 0)),
        ],
        out_specs=pl.BlockSpec((tm, D), lambda i: (i, 0)),
        out_shape=jax.ShapeDtypeStruct((N, D), F32),
        compiler_params=_cparams(("parallel",)),
        name="ple",
    )(h, p, g, pw, gw, fg)


def _layer_params(i, norm1_g, w_in, conv_w, conv_b, lru_wa, lru_ba, lru_wx, lru_bx, lru_lambda,
                  rwkv_mu, rwkv_w0, rwkv_w2, rwkv_a0, rwkv_a2, rwkv_g2, rwkv_kk, rwkv_ka, rwkv_rk,
                  rwkv_ln_w, rwkv_ln_b, w_out, norm2_g, peer_wq, peer_keys, peer_u, peer_v,
                  norm3_g, ple_w, ple_gate):
    row = lambda a: a.reshape(1, -1)
    wi = w_in[i].astype(BF16)
    zpad = jnp.zeros((LORA_WA // 2, RWKV_WIDTH), BF16)
    return dict(
        g1=row(norm1_g[i]), wl=wi[:, :LRU_COLS], wr=wi[:, LRU_COLS:],
        lru=dict(cw=conv_w[i], cb=row(conv_b[i]), wa=lru_wa[i].astype(BF16), ba=row(lru_ba[i]),
                 wx=lru_wx[i].astype(BF16), bx=row(lru_bx[i]), lam=row(lru_lambda[i])),
        rwkv=dict(mu=row(rwkv_mu[i]), w0=row(rwkv_w0[i]),
                  w2=jnp.concatenate([rwkv_w2[i].astype(BF16), zpad], axis=0), a0=row(rwkv_a0[i]),
                  a2=jnp.concatenate([zpad, rwkv_a2[i].astype(BF16)], axis=0),
                  g2=rwkv_g2[i].astype(BF16), kk=row(rwkv_kk[i]), ka=row(rwkv_ka[i]),
                  rk=row(rwkv_rk[i]), lnw=row(rwkv_ln_w[i]), lnb=row(rwkv_ln_b[i])),
        wo_a=w_out[i, :LRU_WIDTH].astype(BF16), wo_b=w_out[i, LRU_WIDTH:].astype(BF16),
        g2=row(norm2_g[i]), wq_t=peer_wq[i].T.astype(BF16),
        keys=peer_keys[i].reshape(2 * PEER_HEADS, N_KEYS, PEER_HALF).astype(BF16),
        u=peer_u[i].astype(BF16), vt=peer_v[i].T.astype(BF16),
        g3=row(norm3_g[i]), pw=ple_w[i].astype(BF16), gw=ple_gate[i].astype(BF16),
    )


def _pairs_from_state(s):
    B = s.shape[0]
    s = s.reshape(B, RWKV_PAIRS, 2, RWKV_HEAD, RWKV_HEAD)
    z = jnp.zeros_like(s[:, :, 0])
    top = jnp.concatenate([s[:, :, 0], z], axis=-1)
    bot = jnp.concatenate([z, s[:, :, 1]], axis=-1)
    return jnp.concatenate([top, bot], axis=-2)


def _state_from_pairs(w):
    B = w.shape[0]
    w = w.reshape(B, RWKV_PAIRS, 2, RWKV_HEAD, 2, RWKV_HEAD)
    return jnp.stack([w[:, :, 0, :, 0, :], w[:, :, 1, :, 1, :]], axis=2).reshape(
        B, RWKV_HEADS, RWKV_HEAD, RWKV_HEAD)


def _trunk(x, p, conv_st, lru_st, shift_st, wkv_st, first, layers, final_g, tiles):
    B, T, D = x.shape
    depth = len(layers)
    tm, lt, tt, ec = tiles
    h = x
    convs, lrus, shifts, wkvs = [], [], [], []
    for i, lp in enumerate(layers):
        zl, zr = _inproj(h, lp["g1"], lp["wl"], lp["wr"], tm)
        conv0 = jnp.swapaxes(conv_st[i], 0, 1).reshape((CONV_W - 1) * B, LRU_WIDTH)
        y_lru, h_lru = _lru(zl.reshape(T * B, LRU_COLS), conv0, lru_st[i], lp["lru"], B, T, lt, first)
        y_rw, wt = _rwkv(zr, shift_st[i].reshape(B, 1, RWKV_COLS), _pairs_from_state(wkv_st[i]),
                         lp["rwkv"], B, T)
        h1 = _outproj(h, y_lru.reshape(T, B * LRU_WIDTH), y_rw, lp["wo_a"], lp["wo_b"], tm)
        h2 = _peer(h1.reshape(B * T, D), lp["g2"], lp["wq_t"], lp["keys"], lp["u"], lp["vt"], tt, ec)
        h3 = _ple(h2, p[i].reshape(B * T, D_PLE), lp["g3"], lp["pw"], lp["gw"], final_g, tt,
                  final=(i == depth - 1))
        h = h3.reshape(B, T, D)
        zl3 = zl.reshape(T, B, LRU_COLS)
        convs.append(jnp.swapaxes(zl3[T - (CONV_W - 1):, :, :LRU_WIDTH], 0, 1))
        lrus.append(h_lru)
        shifts.append(zr.reshape(T, B, RWKV_COLS)[T - 1])
        wkvs.append(_state_from_pairs(wt))
    return h, jnp.stack(convs), jnp.stack(lrus), jnp.stack(shifts), jnp.stack(wkvs)


def kernel(x_prompt, x_sample, p_prompt, p_sample, state_conv, state_lru, state_shift, state_wkv, norm1_g, w_in, conv_w, conv_b, lru_wa, lru_ba, lru_wx, lru_bx, lru_lambda, rwkv_mu, rwkv_w0, rwkv_w2, rwkv_a0, rwkv_a2, rwkv_g2, rwkv_kk, rwkv_ka, rwkv_rk, rwkv_ln_w, rwkv_ln_b, w_out, norm2_g, peer_wq, peer_keys, peer_u, peer_v, norm3_g, ple_w, ple_gate, final_g):
    depth = w_in.shape[0]
    layers = [_layer_params(i, norm1_g, w_in, conv_w, conv_b, lru_wa, lru_ba, lru_wx, lru_bx,
                            lru_lambda, rwkv_mu, rwkv_w0, rwkv_w2, rwkv_a0, rwkv_a2, rwkv_g2,
                            rwkv_kk, rwkv_ka, rwkv_rk, rwkv_ln_w, rwkv_ln_b, w_out, norm2_g,
                            peer_wq, peer_keys, peer_u, peer_v, norm3_g, ple_w, ple_gate)
              for i in range(depth)]
    fg = final_g.reshape(1, -1)
    bp = x_prompt.shape[0]
    dt = x_prompt.dtype
    zc = jnp.zeros((depth, bp, CONV_W - 1, LRU_WIDTH), dt)
    zl = jnp.zeros((depth, bp, LRU_WIDTH), dt)
    zs = jnp.zeros((depth, bp, RWKV_COLS), dt)
    zw = jnp.zeros((depth, bp, RWKV_HEADS, RWKV_HEAD, RWKV_HEAD), dt)
    tp = min(512, x_prompt.shape[1])
    ts = min(512, x_sample.shape[1])
    out_p = _trunk(x_prompt, p_prompt, zc, zl, zs, zw, True, layers, fg,
                   (tp, min(32, x_prompt.shape[1]), 512, 1024))
    out_s = _trunk(x_sample, p_sample, state_conv, state_lru, state_shift, state_wkv, False,
                   layers, fg, (ts, min(32, x_sample.shape[1]), 512, 1024))
    return (out_p[0], out_s[0]) + tuple(out_p[1:]) + tuple(out_s[1:])
```

```python
import functools
import math

import jax
import jax.numpy as jnp
from jax import lax
from jax.experimental import pallas as pl
from jax.experimental.pallas import tpu as pltpu

F32 = jnp.float32
BF16 = jnp.bfloat16

D_MODEL = 1024
D_PLE = 256
LRU_WIDTH = 512
LRU_BLOCKS = 4
LRU_BLOCK = 128
CONV_W = 4
LRU_C = 8.0
RWKV_WIDTH = 512
RWKV_HEAD = 64
RWKV_HEADS = 8
RWKV_PAIRS = RWKV_HEADS // 2
LORA_WA = 128
G_LORA = 128
RWKV_COLS = 3 * RWKV_WIDTH + LORA_WA + G_LORA
LRU_COLS = 2 * LRU_WIDTH
PEER_HEADS = 8
N_KEYS = 128
N_EXPERTS = N_KEYS * N_KEYS
PEER_HALF = 128
PEER_TOPK = 16
PEER_IG = 2
EPS = 1e-6
GN_EPS = 64e-5

RWKV_CHUNK = 128
VMEM_LIMIT = 56 * 1024 * 1024


def _cparams(sem):
    return pltpu.CompilerParams(dimension_semantics=sem, vmem_limit_bytes=VMEM_LIMIT)


def _rms(x, g):
    return x * lax.rsqrt(jnp.mean(x * x, axis=-1, keepdims=True) + EPS) * g


def _gelu(x):
    c = math.sqrt(2.0 / math.pi)
    return x * (0.5 * (1.0 + jnp.tanh(c * (x + 0.044715 * (x * x * x)))))


def _sigmoid(x):
    return 1.0 / (1.0 + jnp.exp(-x))


def _softplus(x):
    return jnp.maximum(x, 0.0) + jnp.log1p(jnp.exp(-jnp.abs(x)))


def _dot(a, b):
    return jnp.dot(a, b, preferred_element_type=F32)


def _dot_nt(a, b):
    return lax.dot_general(a, b, (((1,), (1,)), ((), ())), preferred_element_type=F32)


def _dot_tn(a, b):
    return lax.dot_general(a, b, (((0,), (0,)), ((), ())), preferred_element_type=F32)


def _inproj_kernel(x_ref, g_ref, wl_ref, wr_ref, zl_ref, zr_ref):
    xn = _rms(x_ref[...], g_ref[...]).astype(BF16)
    zl_ref[...] = _dot(xn, wl_ref[...])
    zr_ref[...] = _dot(xn, wr_ref[...])


def _inproj(x, g, wl, wr, tm):
    B, T, D = x.shape
    return pl.pallas_call(
        _inproj_kernel,
        grid=(B, T // tm),
        in_specs=[
            pl.BlockSpec((None, tm, D), lambda b, i: (b, i, 0)),
            pl.BlockSpec((1, D), lambda b, i: (0, 0)),
            pl.BlockSpec((D, LRU_COLS), lambda b, i: (0, 0)),
            pl.BlockSpec((D, RWKV_COLS), lambda b, i: (0, 0)),
        ],
        out_specs=[
            pl.BlockSpec((tm, LRU_COLS), lambda b, i: (i, b)),
            pl.BlockSpec((tm, RWKV_COLS), lambda b, i: (i, b)),
        ],
        out_shape=[
            jax.ShapeDtypeStruct((T, B * LRU_COLS), F32),
            jax.ShapeDtypeStruct((T, B * RWKV_COLS), F32),
        ],
        compiler_params=_cparams(("parallel", "parallel")),
        name="inproj",
    )(x, g, wl, wr)


def _lru_kernel(z_ref, conv0_ref, h0_ref, cw_ref, cb_ref, wa_ref, ba_ref, wx_ref, bx_ref,
                lam_ref, y_ref, hlast_ref, hist_s, h_s, a_s, b_s, *, B, Lt, first):
    i = pl.program_id(0)
    W = LRU_WIDTH
    R = Lt * B

    @pl.when(i == 0)
    def _():
        hist_s[...] = conv0_ref[...]
        h_s[...] = h0_ref[...]

    xb = z_ref[:, 0:W]
    gb = z_ref[:, W:2 * W]
    full = jnp.concatenate([hist_s[...], xb], axis=0)
    xc = cb_ref[...] + full[0:R] * cw_ref[0:1, :]
    for j in range(1, CONV_W):
        xc = xc + full[j * B:j * B + R] * cw_ref[j:j + 1, :]
    hist_s[...] = full[R:R + (CONV_W - 1) * B]

    ra, ix = [], []
    for n in range(LRU_BLOCKS):
        xn = xc[:, n * LRU_BLOCK:(n + 1) * LRU_BLOCK].astype(BF16)
        ra.append(_dot(xn, wa_ref[n]))
        ix.append(_dot(xn, wx_ref[n]))
    r = _sigmoid(jnp.concatenate(ra, axis=1) + ba_ref[...])
    ig = _sigmoid(jnp.concatenate(ix, axis=1) + bx_ref[...])
    log_a = -LRU_C * r * _softplus(-lam_ref[...])
    a = jnp.exp(log_a)
    t = jnp.tanh(log_a)
    mult = jnp.sqrt(-2.0 * t / (1.0 - t))
    if first:
        row = lax.broadcasted_iota(jnp.int32, (R, 1), 0)
        mult = jnp.where(row < jnp.where(i == 0, B, 0), 1.0, mult)
    a_s[...] = a
    b_s[...] = mult * (ig * xc)

    def step(tt, h):
        r0 = pl.multiple_of(tt * B, B)
        h = a_s[pl.ds(r0, B), :] * h + b_s[pl.ds(r0, B), :]
        b_s[pl.ds(r0, B), :] = h
        return h

    h = lax.fori_loop(0, Lt, step, h_s[...], unroll=8)
    h_s[...] = h
    hlast_ref[...] = h
    y_ref[...] = b_s[...] * _gelu(gb)


def _lru(zl, conv0, h0, p, B, T, Lt, first):
    W = LRU_WIDTH
    R = Lt * B
    kern = functools.partial(_lru_kernel, B=B, Lt=Lt, first=first)
    full = lambda shape: pl.BlockSpec(shape, lambda i: (0,) * len(shape))
    return pl.pallas_call(
        kern,
        grid=(T // Lt,),
        in_specs=[
            pl.BlockSpec((R, LRU_COLS), lambda i: (i, 0)),
            full(((CONV_W - 1) * B, W)),
            full((B, W)),
            full((CONV_W, W)),
            full((1, W)),
            full((LRU_BLOCKS, LRU_BLOCK, LRU_BLOCK)),
            full((1, W)),
            full((LRU_BLOCKS, LRU_BLOCK, LRU_BLOCK)),
            full((1, W)),
            full((1, W)),
        ],
        out_specs=[
            pl.BlockSpec((R, W), lambda i: (i, 0)),
            full((B, W)),
        ],
        out_shape=[
            jax.ShapeDtypeStruct((T * B, W), F32),
            jax.ShapeDtypeStruct((B, W), F32),
        ],
        scratch_shapes=[
            pltpu.VMEM(((CONV_W - 1) * B, W), F32),
            pltpu.VMEM((B, W), F32),
            pltpu.VMEM((R, W), F32),
            pltpu.VMEM((R, W), F32),
        ],
        compiler_params=_cparams(("arbitrary",)),
        name="rglru",
    )(zl, conv0, h0, p["cw"], p["cb"], p["wa"], p["ba"], p["wx"], p["bx"], p["lam"])


def _rwkv_kernel(z_ref, sh0_ref, wt0_ref, mu_ref, w0_ref, w2_ref, a0_ref, a2_ref, g2_ref,
                 kk_ref, ka_ref, rk_ref, lnw_ref, lnb_ref, y_ref, wt_ref, zprev_s, wt_s, *, Tv):
    L = RWKV_CHUNK
    Wd = RWKV_WIDTH
    c = pl.program_id(1)

    @pl.when(c == 0)
    def _():
        zprev_s[...] = sh0_ref[...]
        wt_s[...] = wt0_ref[...]

    z = z_ref[...]
    if Tv < L:
        z = jnp.concatenate([z, jnp.zeros((L - Tv, RWKV_COLS), F32)], axis=0)
    row = lax.broadcasted_iota(jnp.int32, (L, 1), 0)
    zp = jnp.where(row == 0, zprev_s[...], pltpu.roll(z, 1, 0))
    zprev_s[...] = z[Tv - 1:Tv, :]
    zs = z + (zp - z) * mu_ref[...]
    r = zs[:, 0:Wd]
    k = zs[:, Wd:2 * Wd]
    v = zs[:, 2 * Wd:3 * Wd]
    zwa = zs[:, 3 * Wd:3 * Wd + LORA_WA]
    zg = zs[:, 3 * Wd + LORA_WA:]
    if Tv < L:
        valid = row < Tv
        r = jnp.where(valid, r, 0.0)
        k = jnp.where(valid, k, 0.0)
        v = jnp.where(valid, v, 0.0)
    w_log = -_softplus(-(w0_ref[...] + _dot(jnp.tanh(zwa).astype(BF16), w2_ref[...]))) - 0.5
    ld = -jnp.exp(w_log)
    if Tv < L:
        ld = jnp.where(valid, ld, 0.0)
    iclr = _sigmoid(a0_ref[...] + _dot(zwa.astype(BF16), a2_ref[...]))
    g = _dot(_sigmoid(zg).astype(BF16), g2_ref[...])
    kkraw = k * kk_ref[...]
    kmod = k * (1.0 + (iclr - 1.0) * ka_ref[...])

    ri = lax.broadcasted_iota(jnp.int32, (L, L), 0)
    ci = lax.broadcasted_iota(jnp.int32, (L, L), 1)
    strict = ri > ci
    incl = ri >= ci
    tri = jnp.where(incl, 1.0, 0.0).astype(BF16)
    ld_hi = ld.astype(BF16)
    rem = ld - ld_hi.astype(F32)
    ld_mid = rem.astype(BF16)
    ld_lo = (rem - ld_mid.astype(F32)).astype(BF16)
    cum = _dot(tri, ld_hi) + _dot(tri, ld_mid) + _dot(tri, ld_lo)
    cum_l = cum[L - 1:L, :]
    cumm = cum - cum[L // 2 - 1:L // 2, :]
    e_m = jnp.exp(cumm)
    e_mi = jnp.exp(-cumm)
    e_mprev = jnp.exp(cumm - ld)
    e_0 = jnp.exp(cum)
    e_0prev = jnp.exp(cum - ld)
    e_end = jnp.exp(cum_l - cum)
    p_l = jnp.exp(cum_l)
    lane = lax.broadcasted_iota(jnp.int32, (1, 2 * RWKV_HEAD), 1)
    m0 = lane < RWKV_HEAD
    bi = lax.broadcasted_iota(jnp.int32, (2 * RWKV_HEAD, 2 * RWKV_HEAD), 0) < RWKV_HEAD
    bj = lax.broadcasted_iota(jnp.int32, (2 * RWKV_HEAD, 2 * RWKV_HEAD), 1) < RWKV_HEAD
    blockdiag = bi == bj

    def headsum(x):
        s0 = jnp.sum(jnp.where(m0, x, 0.0), axis=-1, keepdims=True)
        s1 = jnp.sum(jnp.where(m0, 0.0, x), axis=-1, keepdims=True)
        return jnp.where(m0, s0, s1)

    P = range(RWKV_PAIRS)
    sls = [slice(p * 128, (p + 1) * 128) for p in P]
    pre = []
    for p in P:
        sl = sls[p]
        r_p, v_p, kmod_p, iclr_p = r[:, sl], v[:, sl], kmod[:, sl], iclr[:, sl]
        kkr = kkraw[:, sl]
        kk = kkr / jnp.maximum(jnp.sqrt(headsum(kkr * kkr)), 1e-12)
        a_p = -kk
        b_p = kk * iclr_p
        at = a_p * e_mprev[:, sl]
        rt = r_p * e_m[:, sl]
        bk = jnp.concatenate([b_p * e_mi[:, sl], kmod_p * e_mi[:, sl]], axis=0).astype(BF16)
        ar0 = jnp.concatenate([a_p * e_0prev[:, sl], r_p * e_0[:, sl]], axis=0).astype(BF16)
        be = jnp.concatenate([b_p * e_end[:, sl], kmod_p * e_end[:, sl]], axis=0).astype(BF16)
        pre.append(dict(r=r_p, v=v_p, kmod=kmod_p, at=at, rt=rt, bk=bk, ar0=ar0, be=be,
                        vb=v_p.astype(BF16)))
    wts = [wt_s[p] for p in P]
    as0 = [_dot_nt(pre[p]["ar0"], wts[p].astype(BF16)) for p in P]
    heads = [(p, h) for p in P for h in range(2)]
    gms = {}
    for (p, h) in heads:
        mh = m0 if h == 0 else jnp.logical_not(m0)
        arm = jnp.concatenate([jnp.where(mh, pre[p]["at"], 0.0), jnp.where(mh, pre[p]["rt"], 0.0)],
                              axis=0)
        gms[(p, h)] = _dot_nt(arm.astype(BF16), pre[p]["bk"])
    nab, mm, ys = {}, {}, {}
    for (p, h) in heads:
        gm = gms[(p, h)]
        nab[(p, h)] = jnp.where(strict, gm[0:L, 0:L], 0.0).astype(BF16)
        n_ak = jnp.where(strict, gm[0:L, L:2 * L], 0.0)
        mm[(p, h)] = jnp.concatenate([jnp.where(incl, gm[L:2 * L, 0:L], 0.0),
                                      jnp.where(incl, gm[L:2 * L, L:2 * L], 0.0)],
                                     axis=1).astype(BF16)
        ys[(p, h)] = as0[p][0:L] + _dot(n_ak.astype(BF16), pre[p]["vb"])
    pws = dict(nab)
    nlev = int(math.log2(L))
    for lev in range(nlev):
        for hd in heads:
            ys[hd] = ys[hd] + _dot(pws[hd], ys[hd].astype(BF16))
        if lev + 1 < nlev:
            for hd in heads:
                pws[hd] = _dot(pws[hd], pws[hd]).astype(BF16)
    for p in P:
        sl = sls[p]
        pp = pre[p]
        u_p = jnp.where(m0, ys[(p, 0)], ys[(p, 1)])
        uv = jnp.concatenate([u_p, pp["v"]], axis=0).astype(BF16)
        o0 = as0[p][L:2 * L] + _dot(mm[(p, 0)], uv)
        o1 = as0[p][L:2 * L] + _dot(mm[(p, 1)], uv)
        o = jnp.where(m0, o0, o1)
        wt_new = wts[p] * p_l[:, sl] + jnp.where(blockdiag, _dot_tn(uv, pp["be"]), 0.0)
        wt_s[p] = wt_new
        wt_ref[p] = wt_new
        mean = headsum(o) * (1.0 / RWKV_HEAD)
        d = o - mean
        var = headsum(d * d) * (1.0 / RWKV_HEAD)
        on = d * lax.rsqrt(var + GN_EPS) * lnw_ref[:, sl] + lnb_ref[:, sl]
        bonus = headsum(pp["r"] * pp["kmod"] * rk_ref[:, sl]) * pp["v"]
        y = (on + bonus) * g[:, sl]
        y_ref[:, sl] = y[0:Tv]


def _rwkv(zr, sh0, wt0, p, B, T):
    Tv = min(T, RWKV_CHUNK)
    kern = functools.partial(_rwkv_kernel, Tv=Tv)
    row = lambda n: pl.BlockSpec((1, n), lambda b, c: (0, 0))
    mat = lambda m, n: pl.BlockSpec((m, n), lambda b, c: (0, 0))
    st = pl.BlockSpec((None, RWKV_PAIRS, 128, 128), lambda b, c: (b, 0, 0, 0))
    return pl.pallas_call(
        kern,
        grid=(B, T // Tv),
        in_specs=[
            pl.BlockSpec((Tv, RWKV_COLS), lambda b, c: (c, b)),
            pl.BlockSpec((None, 1, RWKV_COLS), lambda b, c: (b, 0, 0)),
            st,
            row(RWKV_COLS), row(RWKV_WIDTH), mat(LORA_WA, RWKV_WIDTH), row(RWKV_WIDTH),
            mat(LORA_WA, RWKV_WIDTH), mat(G_LORA, RWKV_WIDTH),
            row(RWKV_WIDTH), row(RWKV_WIDTH), row(RWKV_WIDTH), row(RWKV_WIDTH), row(RWKV_WIDTH),
        ],
        out_specs=[
            pl.BlockSpec((Tv, RWKV_WIDTH), lambda b, c: (c, b)),
            st,
        ],
        out_shape=[
            jax.ShapeDtypeStruct((T, B * RWKV_WIDTH), F32),
            jax.ShapeDtypeStruct((B, RWKV_PAIRS, 128, 128), F32),
        ],
        scratch_shapes=[
            pltpu.VMEM((1, RWKV_COLS), F32),
            pltpu.VMEM((RWKV_PAIRS, 128, 128), F32),
        ],
        compiler_params=_cparams(("arbitrary", "arbitrary")),
        name="rwkv7",
    )(zr, sh0, wt0, p["mu"], p["w0"], p["w2"], p["a0"], p["a2"], p["g2"],
      p["kk"], p["ka"], p["rk"], p["lnw"], p["lnb"])


def _outproj_kernel(x_ref, yl_ref, yr_ref, wa_ref, wb_ref, o_ref):
    o_ref[...] = (x_ref[...] + _dot(yl_ref[...].astype(BF16), wa_ref[...])
                  + _dot(yr_ref[...].astype(BF16), wb_ref[...]))


def _outproj(x, yl, yr, wa, wb, tm):
    B, T, D = x.shape
    return pl.pallas_call(
        _outproj_kernel,
        grid=(B, T // tm),
        in_specs=[
            pl.BlockSpec((None, tm, D), lambda b, i: (b, i, 0)),
            pl.BlockSpec((tm, LRU_WIDTH), lambda b, i: (i, b)),
            pl.BlockSpec((tm, RWKV_WIDTH), lambda b, i: (i, b)),
            pl.BlockSpec((LRU_WIDTH, D), lambda b, i: (0, 0)),
            pl.BlockSpec((RWKV_WIDTH, D), lambda b, i: (0, 0)),
        ],
        out_specs=pl.BlockSpec((None, tm, D), lambda b, i: (b, i, 0)),
        out_shape=jax.ShapeDtypeStruct((B, T, D), F32),
        compiler_params=_cparams(("parallel", "parallel")),
        name="outproj",
    )(x, yl, yr, wa, wb)


def _sort16_pairs():
    pairs = []

    def merge(lo, n, r):
        step = 2 * r
        if step < n:
            merge(lo, n, step)
            merge(lo + r, n, step)
            for i in range(lo + r, lo + n - r, step):
                pairs.append((i, i + r))
        else:
            pairs.append((lo, lo + r))

    def sort(lo, n):
        if n > 1:
            sort(lo, n // 2)
            sort(lo + n // 2, n // 2)
            merge(lo, n, 1)

    sort(0, 16)
    return pairs


_SORT16 = _sort16_pairs()


def _merge_top(g, n):
    g = list(g)
    S = g[0].shape[0]
    sub = lax.broadcasted_iota(jnp.int32, g[0].shape, 0).astype(F32)
    rows = []
    for it in range(n):
        m = jnp.max(g[0], axis=0, keepdims=True)
        rows.append(m)
        if it + 1 < n:
            first = jnp.min(jnp.where(g[0] == m, sub, float(S)), axis=0, keepdims=True)
            sel = sub == first
            for v in range(min(n - 1 - it, len(g) - 1)):
                g[v] = jnp.where(sel, g[v + 1], g[v])
            if len(g) <= n - 1 - it:
                g[-1] = jnp.where(sel, -jnp.inf, g[-1])
    return rows


def _top16_rows(x):
    g = [x[8 * v:8 * v + 8, :] for v in range(16)]
    for (i, j) in _SORT16:
        g[i], g[j] = jnp.maximum(g[i], g[j]), jnp.minimum(g[i], g[j])
    return _merge_top(g, PEER_TOPK)


def _gelu2(x):
    c = math.sqrt(2.0 / math.pi)
    inner = x * (c + (c * 0.044715) * (x * x))
    hx = 0.5 * x
    return hx + hx * jnp.tanh(inner)


def _zero_like_dep(x):
    u = pltpu.bitcast(x, jnp.uint32)
    z = lax.shift_right_logical(lax.shift_right_logical(u, jnp.uint32(16)), jnp.uint32(16))
    return pltpu.bitcast(z, F32)


def _add_dep(a, z):
    zb = jnp.concatenate([z, z], axis=0).astype(BF16)
    return a + jnp.tile(zb, (a.shape[0] // 16, a.shape[1] // 128))


def _peer_kernel(h_ref, g_ref, wq_ref, keys_ref, u_ref, vt_ref, o_ref,
                 xnt_s, s1_s, s2_s, tau_s, s1x_s, w_s, acc_s, *, Tt, Ec):
    j = pl.program_id(1)
    nI = Ec // N_KEYS
    nT = Tt // 128
    Kk = PEER_TOPK
    log2e = 1.0 / math.log(2.0)

    @pl.when(j == 0)
    def _prep():
        xn = _rms(h_ref[...], g_ref[...])
        xnb = xn.astype(BF16)
        xnt_s[...] = xn.T.astype(BF16)
        acc_s[...] = jnp.zeros_like(acc_s)
        sub = lax.broadcasted_iota(jnp.int32, (Kk, Tt), 0)

        def cands(a1, t2):
            return [jnp.where(sub < Kk // (q + 1), a1 + t2[q], -jnp.inf) for q in range(Kk)]

        for h in range(PEER_HEADS):
            q = _dot_nt(wq_ref[h * 2 * PEER_HALF:(h + 1) * 2 * PEER_HALF, :], xnb)
            s1 = _dot(keys_ref[2 * h], q[0:PEER_HALF].astype(BF16))
            s2 = _dot(keys_ref[2 * h + 1], q[PEER_HALF:].astype(BF16))
            t1 = _top16_rows(s1)
            t2 = _top16_rows(s2)
            sc1 = (s1 - t1[0]) * log2e
            sc2 = (s2 - t2[0]) * log2e
            a1 = (jnp.concatenate(t1, axis=0) - t1[0]) * log2e
            a2 = [(t - t2[0]) * log2e for t in t2]
            zsum = jnp.zeros_like(t1[0])
            for b_ in _merge_top(cands(a1, a2), Kk):
                zsum = zsum + jnp.exp2(b_)
            lz = jnp.log2(zsum)
            sc2 = sc2 - lz
            a2 = [t - lz for t in a2]
            tau_s[h:h + 1, :] = _merge_top(cands(a1, a2), Kk)[Kk - 1]
            for tb in range(nT):
                cs = slice(tb * 128, (tb + 1) * 128)
                s1_s[h, tb] = sc1[:, cs]
                s2_s[h, tb] = sc2[:, cs]

    HK = N_KEYS // 2
    IG = PEER_IG
    GR = IG * N_KEYS
    r0d = pl.multiple_of(j * nI, nI)
    for h in range(PEER_HEADS):
        for tb in range(nT):
            s1x_s[h, tb] = s1_s[h, tb, pl.ds(r0d, nI), :]

    def hid_piece(ig, z):
        lhs = u_ref[ig * N_KEYS:ig * N_KEYS + GR, :]
        if z is not None:
            lhs = _add_dep(lhs, z)
        return _dot(lhs, xnt_s[...])

    groups = list(range(0, nI, IG))
    zc = jnp.zeros((8, 128), F32)
    hid = hid_piece(groups[0], None)
    for gi, ig in enumerate(groups):
        hid_next = None
        for tb in range(nT):
            cs = slice(tb * 128, (tb + 1) * 128)
            for half in range(2):
                ks = slice(half * HK, (half + 1) * HK)
                accs = [jnp.concatenate([zc] * (HK // 8), axis=0) for _ in range(IG)]
                for h in range(PEER_HEADS):
                    s2h = s2_s[h, tb, ks, :]
                    tau = tau_s[h:h + 1, cs]
                    for ii in range(IG):
                        i1 = ig + ii
                        cval = s1x_s[h, tb, i1:i1 + 1, :] + s2h
                        accs[ii] = accs[ii] + jnp.where(cval >= tau, jnp.exp2(cval), 0.0)
                zc = _zero_like_dep(accs[0][0:8, :])
                if tb == 0 and half == 0 and gi + 1 < len(groups):
                    hid_next = hid_piece(groups[gi + 1], zc)
                for ii in range(IG):
                    r0 = (ig + ii) * N_KEYS + half * HK
                    hr = slice(ii * N_KEYS + half * HK, ii * N_KEYS + (half + 1) * HK)
                    w_s[r0:r0 + HK, cs] = accs[ii].astype(BF16) * _gelu2(hid[hr, cs].astype(BF16))
        if gi % 2 == 1:
            kr = slice((ig - IG) * N_KEYS, (ig + IG) * N_KEYS)
            acc_s[...] += _dot(vt_ref[:, kr], w_s[kr, :])
        hid = hid_next

    @pl.when(j == pl.num_programs(1) - 1)
    def _fin():
        o_ref[...] = h_ref[...] + acc_s[...].T


def _peer(h, g, wq_t, keys, u, vt, Tt, Ec):
    N, D = h.shape
    nT = Tt // 128
    nI = Ec // N_KEYS
    kern = functools.partial(_peer_kernel, Tt=Tt, Ec=Ec)
    score = pltpu.VMEM((PEER_HEADS, nT, N_KEYS, 128), F32)
    rows = pltpu.VMEM((PEER_HEADS, nT, nI, 128), F32)
    return pl.pallas_call(
        kern,
        grid=(N // Tt, N_EXPERTS // Ec),
        in_specs=[
            pl.BlockSpec((Tt, D), lambda i, j: (i, 0)),
            pl.BlockSpec((1, D), lambda i, j: (0, 0)),
            pl.BlockSpec((PEER_HEADS * 2 * PEER_HALF, D), lambda i, j: (0, 0)),
            pl.BlockSpec((2 * PEER_HEADS, N_KEYS, PEER_HALF), lambda i, j: (0, 0, 0)),
            pl.BlockSpec((Ec, D), lambda i, j: (j, 0)),
            pl.BlockSpec((D, Ec), lambda i, j: (0, j)),
        ],
        out_specs=pl.BlockSpec((Tt, D), lambda i, j: (i, 0)),
        out_shape=jax.ShapeDtypeStruct((N, D), F32),
        scratch_shapes=[
            pltpu.VMEM((D, Tt), BF16),
            score, score,
            pltpu.VMEM((PEER_HEADS, Tt), F32),
            rows,
            pltpu.VMEM((Ec, Tt), BF16),
            pltpu.VMEM((D, Tt), F32),
        ],
        compiler_params=_cparams(("parallel", "arbitrary")),
        name="peer",
    )(h, g, wq_t, keys, u, vt)


def _ple_kernel(h_ref, p_ref, g_ref, pw_ref, gw_ref, fg_ref, o_ref, *, final):
    h = h_ref[...]
    gate = _sigmoid(_dot(_rms(h, g_ref[...]).astype(BF16), gw_ref[...]))
    out = h + _dot(p_ref[...].astype(BF16), pw_ref[...]) * gate
    if final:
        out = _rms(out, fg_ref[...])
    o_ref[...] = out


def _ple(h, p, g, pw, gw, fg, tm, final):
    N, D = h.shape
    return pl.pallas_call(
        functools.partial(_ple_kernel, final=final),
        grid=(N // tm,),
        in_specs=[
            pl.BlockSpec((tm, D), lambda i: (i, 0)),
            pl.BlockSpec((tm, D_PLE), lambda i: (i, 0)),
            pl.BlockSpec((1, D), lambda i: (0, 0)),
            pl.BlockSpec((D_PLE, D), lambda i: (0, 0)),
            pl.BlockSpec((D, D), lambda i: (0, 0)),
            pl.BlockSpec((1, D), lambda i: (0, 0)),
        ],
        out_specs=pl.BlockSpec((tm, D), lambda i: (i, 0)),
        out_shape=jax.ShapeDtypeStruct((N, D), F32),
        compiler_params=_cparams(("parallel",)),
        name="ple",
    )(h, p, g, pw, gw, fg)


def _layer_params(i, norm1_g, w_in, conv_w, conv_b, lru_wa, lru_ba, lru_wx, lru_bx, lru_lambda,
                  rwkv_mu, rwkv_w0, rwkv_w2, rwkv_a0, rwkv_a2, rwkv_g2, rwkv_kk, rwkv_ka, rwkv_rk,
                  rwkv_ln_w, rwkv_ln_b, w_out, norm2_g, peer_wq, peer_keys, peer_u, peer_v,
                  norm3_g, ple_w, ple_gate):
    row = lambda a: a.reshape(1, -1)
    wi = w_in[i].astype(BF16)
    zpad = jnp.zeros((LORA_WA // 2, RWKV_WIDTH), BF16)
    return dict(
        g1=row(norm1_g[i]), wl=wi[:, :LRU_COLS], wr=wi[:, LRU_COLS:],
        lru=dict(cw=conv_w[i], cb=row(conv_b[i]), wa=lru_wa[i].astype(BF16), ba=row(lru_ba[i]),
                 wx=lru_wx[i].astype(BF16), bx=row(lru_bx[i]), lam=row(lru_lambda[i])),
        rwkv=dict(mu=row(rwkv_mu[i]), w0=row(rwkv_w0[i]),
                  w2=jnp.concatenate([rwkv_w2[i].astype(BF16), zpad], axis=0), a0=row(rwkv_a0[i]),
                  a2=jnp.concatenate([zpad, rwkv_a2[i].astype(BF16)], axis=0),
                  g2=rwkv_g2[i].astype(BF16), kk=row(rwkv_kk[i]), ka=row(rwkv_ka[i]),
                  rk=row(rwkv_rk[i]), lnw=row(rwkv_ln_w[i]), lnb=row(rwkv_ln_b[i])),
        wo_a=w_out[i, :LRU_WIDTH].astype(BF16), wo_b=w_out[i, LRU_WIDTH:].astype(BF16),
        g2=row(norm2_g[i]), wq_t=peer_wq[i].T.astype(BF16),
        keys=peer_keys[i].reshape(2 * PEER_HEADS, N_KEYS, PEER_HALF).astype(BF16),
        u=peer_u[i].astype(BF16), vt=peer_v[i].T.astype(BF16),
        g3=row(norm3_g[i]), pw=ple_w[i].astype(BF16), gw=ple_gate[i].astype(BF16),
    )


def _pairs_from_state(s):
    B = s.shape[0]
    s = s.reshape(B, RWKV_PAIRS, 2, RWKV_HEAD, RWKV_HEAD)
    z = jnp.zeros_like(s[:, :, 0])
    top = jnp.concatenate([s[:, :, 0], z], axis=-1)
    bot = jnp.concatenate([z, s[:, :, 1]], axis=-1)
    return jnp.concatenate([top, bot], axis=-2)


def _state_from_pairs(w):
    B = w.shape[0]
    w = w.reshape(B, RWKV_PAIRS, 2, RWKV_HEAD, 2, RWKV_HEAD)
    return jnp.stack([w[:, :, 0, :, 0, :], w[:, :, 1, :, 1, :]], axis=2).reshape(
        B, RWKV_HEADS, RWKV_HEAD, RWKV_HEAD)


def _trunk(x, p, conv_st, lru_st, shift_st, wkv_st, first, layers, final_g, tiles):
    B, T, D = x.shape
    depth = len(layers)
    tm, lt, tt, ec = tiles
    h = x
    convs, lrus, shifts, wkvs = [], [], [], []
    for i, lp in enumerate(layers):
        zl, zr = _inproj(h, lp["g1"], lp["wl"], lp["wr"], tm)
        conv0 = jnp.swapaxes(conv_st[i], 0, 1).reshape((CONV_W - 1) * B, LRU_WIDTH)
        y_lru, h_lru = _lru(zl.reshape(T * B, LRU_COLS), conv0, lru_st[i], lp["lru"], B, T, lt, first)
        y_rw, wt = _rwkv(zr, shift_st[i].reshape(B, 1, RWKV_COLS), _pairs_from_state(wkv_st[i]),
                         lp["rwkv"], B, T)
        h1 = _outproj(h, y_lru.reshape(T, B * LRU_WIDTH), y_rw, lp["wo_a"], lp["wo_b"], tm)
        h2 = _peer(h1.reshape(B * T, D), lp["g2"], lp["wq_t"], lp["keys"], lp["u"], lp["vt"], tt, ec)
        h3 = _ple(h2, p[i].reshape(B * T, D_PLE), lp["g3"], lp["pw"], lp["gw"], final_g, tt,
                  final=(i == depth - 1))
        h = h3.reshape(B, T, D)
        zl3 = zl.reshape(T, B, LRU_COLS)
        convs.append(jnp.swapaxes(zl3[T - (CONV_W - 1):, :, :LRU_WIDTH], 0, 1))
        lrus.append(h_lru)
        shifts.append(zr.reshape(T, B, RWKV_COLS)[T - 1])
        wkvs.append(_state_from_pairs(wt))
    return h, jnp.stack(convs), jnp.stack(lrus), jnp.stack(shifts), jnp.stack(wkvs)


def kernel(x_prompt, x_sample, p_prompt, p_sample, state_conv, state_lru, state_shift, state_wkv, norm1_g, w_in, conv_w, conv_b, lru_wa, lru_ba, lru_wx, lru_bx, lru_lambda, rwkv_mu, rwkv_w0, rwkv_w2, rwkv_a0, rwkv_a2, rwkv_g2, rwkv_kk, rwkv_ka, rwkv_rk, rwkv_ln_w, rwkv_ln_b, w_out, norm2_g, peer_wq, peer_keys, peer_u, peer_v, norm3_g, ple_w, ple_gate, final_g):
    depth = w_in.shape[0]
    layers = [_layer_params(i, norm1_g, w_in, conv_w, conv_b, lru_wa, lru_ba, lru_wx, lru_bx,
                            lru_lambda, rwkv_mu, rwkv_w0, rwkv_w2, rwkv_a0, rwkv_a2, rwkv_g2,
                            rwkv_kk, rwkv_ka, rwkv_rk, rwkv_ln_w, rwkv_ln_b, w_out, norm2_g,
                            peer_wq, peer_keys, peer_u, peer_v, norm3_g, ple_w, ple_gate)
              for i in range(depth)]
    fg = final_g.reshape(1, -1)
    bp = x_prompt.shape[0]
    dt = x_prompt.dtype
    zc = jnp.zeros((depth, bp, CONV_W - 1, LRU_WIDTH), dt)
    zl = jnp.zeros((depth, bp, LRU_WIDTH), dt)
    zs = jnp.zeros((depth, bp, RWKV_COLS), dt)
    zw = jnp.zeros((depth, bp, RWKV_HEADS, RWKV_HEAD, RWKV_HEAD), dt)
    tp = min(512, x_prompt.shape[1])
    ts = min(512, x_sample.shape[1])
    out_p = _trunk(x_prompt, p_prompt, zc, zl, zs, zw, True, layers, fg,
                   (tp, min(32, x_prompt.shape[1]), 512, 1024))
    out_s = _trunk(x_sample, p_sample, state_conv, state_lru, state_shift, state_wkv, False,
                   layers, fg, (ts, min(32, x_sample.shape[1]), 512, 1024))
    return (out_p[0], out_s[0]) + tuple(out_p[1:]) + tuple(out_s[1:])
```

```python
import functools
import math

import jax
import jax.numpy as jnp
from jax import lax
from jax.experimental import pallas as pl
from jax.experimental.pallas import tpu as pltpu

F32 = jnp.float32
BF16 = jnp.bfloat16

D_MODEL = 1024
D_PLE = 256
LRU_WIDTH = 512
LRU_BLOCKS = 4
LRU_BLOCK = 128
CONV_W = 4
LRU_C = 8.0
RWKV_WIDTH = 512
RWKV_HEAD = 64
RWKV_HEADS = 8
RWKV_PAIRS = RWKV_HEADS // 2
LORA_WA = 128
G_LORA = 128
RWKV_COLS = 3 * RWKV_WIDTH + LORA_WA + G_LORA
LRU_COLS = 2 * LRU_WIDTH
PEER_HEADS = 8
N_KEYS = 128
N_EXPERTS = N_KEYS * N_KEYS
PEER_HALF = 128
PEER_TOPK = 16
PEER_IG = 2
PEER_TT = 512
PEER_EC = 1024
EPS = 1e-6
GN_EPS = 64e-5

RWKV_CHUNK = 128
VMEM_LIMIT = 56 * 1024 * 1024


def _cparams(sem):
    return pltpu.CompilerParams(dimension_semantics=sem, vmem_limit_bytes=VMEM_LIMIT)


def _rms(x, g):
    return x * lax.rsqrt(jnp.mean(x * x, axis=-1, keepdims=True) + EPS) * g


def _gelu(x):
    c = math.sqrt(2.0 / math.pi)
    return x * (0.5 * (1.0 + jnp.tanh(c * (x + 0.044715 * (x * x * x)))))


def _sigmoid(x):
    return 1.0 / (1.0 + jnp.exp(-x))


def _softplus(x):
    return jnp.maximum(x, 0.0) + jnp.log1p(jnp.exp(-jnp.abs(x)))


def _dot(a, b):
    return jnp.dot(a, b, preferred_element_type=F32)


def _dot_nt(a, b):
    return lax.dot_general(a, b, (((1,), (1,)), ((), ())), preferred_element_type=F32)


def _dot_tn(a, b):
    return lax.dot_general(a, b, (((0,), (0,)), ((), ())), preferred_element_type=F32)


def _inproj_kernel(x_ref, g_ref, wl_ref, wr_ref, zl_ref, zr_ref):
    xn = _rms(x_ref[...], g_ref[...]).astype(BF16)
    zl_ref[...] = _dot(xn, wl_ref[...])
    zr_ref[...] = _dot(xn, wr_ref[...])


def _inproj(x, g, wl, wr, tm):
    B, T, D = x.shape
    return pl.pallas_call(
        _inproj_kernel,
        grid=(B, T // tm),
        in_specs=[
            pl.BlockSpec((None, tm, D), lambda b, i: (b, i, 0)),
            pl.BlockSpec((1, D), lambda b, i: (0, 0)),
            pl.BlockSpec((D, LRU_COLS), lambda b, i: (0, 0)),
            pl.BlockSpec((D, RWKV_COLS), lambda b, i: (0, 0)),
        ],
        out_specs=[
            pl.BlockSpec((tm, LRU_COLS), lambda b, i: (i, b)),
            pl.BlockSpec((tm, RWKV_COLS), lambda b, i: (i, b)),
        ],
        out_shape=[
            jax.ShapeDtypeStruct((T, B * LRU_COLS), F32),
            jax.ShapeDtypeStruct((T, B * RWKV_COLS), F32),
        ],
        compiler_params=_cparams(("parallel", "parallel")),
        name="inproj",
    )(x, g, wl, wr)


def _lru_kernel(z_ref, conv0_ref, h0_ref, cw_ref, cb_ref, wa_ref, ba_ref, wx_ref, bx_ref,
                lam_ref, y_ref, hlast_ref, hist_s, h_s, a_s, b_s, *, B, Lt, first):
    i = pl.program_id(0)
    W = LRU_WIDTH
    R = Lt * B

    @pl.when(i == 0)
    def _():
        hist_s[...] = conv0_ref[...]
        h_s[...] = h0_ref[...]

    xb = z_ref[:, 0:W]
    gb = z_ref[:, W:2 * W]
    full = jnp.concatenate([hist_s[...], xb], axis=0)
    xc = cb_ref[...] + full[0:R] * cw_ref[0:1, :]
    for j in range(1, CONV_W):
        xc = xc + full[j * B:j * B + R] * cw_ref[j:j + 1, :]
    hist_s[...] = full[R:R + (CONV_W - 1) * B]

    ra, ix = [], []
    for n in range(LRU_BLOCKS):
        xn = xc[:, n * LRU_BLOCK:(n + 1) * LRU_BLOCK].astype(BF16)
        ra.append(_dot(xn, wa_ref[n]))
        ix.append(_dot(xn, wx_ref[n]))
    r = _sigmoid(jnp.concatenate(ra, axis=1) + ba_ref[...])
    ig = _sigmoid(jnp.concatenate(ix, axis=1) + bx_ref[...])
    log_a = -LRU_C * r * _softplus(-lam_ref[...])
    a = jnp.exp(log_a)
    t = jnp.tanh(log_a)
    mult = jnp.sqrt(-2.0 * t / (1.0 - t))
    if first:
        row = lax.broadcasted_iota(jnp.int32, (R, 1), 0)
        mult = jnp.where(row < jnp.where(i == 0, B, 0), 1.0, mult)
    a_s[...] = a
    b_s[...] = mult * (ig * xc)

    def step(tt, h):
        r0 = pl.multiple_of(tt * B, B)
        h = a_s[pl.ds(r0, B), :] * h + b_s[pl.ds(r0, B), :]
        b_s[pl.ds(r0, B), :] = h
        return h

    h = lax.fori_loop(0, Lt, step, h_s[...], unroll=8)
    h_s[...] = h
    hlast_ref[...] = h
    y_ref[...] = b_s[...] * _gelu(gb)


def _lru(zl, conv0, h0, p, B, T, Lt, first):
    W = LRU_WIDTH
    R = Lt * B
    kern = functools.partial(_lru_kernel, B=B, Lt=Lt, first=first)
    full = lambda shape: pl.BlockSpec(shape, lambda i: (0,) * len(shape))
    return pl.pallas_call(
        kern,
        grid=(T // Lt,),
        in_specs=[
            pl.BlockSpec((R, LRU_COLS), lambda i: (i, 0)),
            full(((CONV_W - 1) * B, W)),
            full((B, W)),
            full((CONV_W, W)),
            full((1, W)),
            full((LRU_BLOCKS, LRU_BLOCK, LRU_BLOCK)),
            full((1, W)),
            full((LRU_BLOCKS, LRU_BLOCK, LRU_BLOCK)),
            full((1, W)),
            full((1, W)),
        ],
        out_specs=[
            pl.BlockSpec((R, W), lambda i: (i, 0)),
            full((B, W)),
        ],
        out_shape=[
            jax.ShapeDtypeStruct((T * B, W), F32),
            jax.ShapeDtypeStruct((B, W), F32),
        ],
        scratch_shapes=[
            pltpu.VMEM(((CONV_W - 1) * B, W), F32),
            pltpu.VMEM((B, W), F32),
            pltpu.VMEM((R, W), F32),
            pltpu.VMEM((R, W), F32),
        ],
        compiler_params=_cparams(("arbitrary",)),
        name="rglru",
    )(zl, conv0, h0, p["cw"], p["cb"], p["wa"], p["ba"], p["wx"], p["bx"], p["lam"])


def _rwkv_kernel(z_ref, sh0_ref, wt0_ref, mu_ref, w0_ref, w2_ref, a0_ref, a2_ref, g2_ref,
                 kk_ref, ka_ref, rk_ref, lnw_ref, lnb_ref, y_ref, wt_ref, zprev_s, wt_s, *, Tv):
    L = RWKV_CHUNK
    Wd = RWKV_WIDTH
    c = pl.program_id(1)

    @pl.when(c == 0)
    def _():
        zprev_s[...] = sh0_ref[...]
        wt_s[...] = wt0_ref[...]

    z = z_ref[...]
    if Tv < L:
        z = jnp.concatenate([z, jnp.zeros((L - Tv, RWKV_COLS), F32)], axis=0)
    row = lax.broadcasted_iota(jnp.int32, (L, 1), 0)
    zp = jnp.where(row == 0, zprev_s[...], pltpu.roll(z, 1, 0))
    zprev_s[...] = z[Tv - 1:Tv, :]
    zs = z + (zp - z) * mu_ref[...]
    r = zs[:, 0:Wd]
    k = zs[:, Wd:2 * Wd]
    v = zs[:, 2 * Wd:3 * Wd]
    zwa = zs[:, 3 * Wd:3 * Wd + LORA_WA]
    zg = zs[:, 3 * Wd + LORA_WA:]
    if Tv < L:
        valid = row < Tv
        r = jnp.where(valid, r, 0.0)
        k = jnp.where(valid, k, 0.0)
        v = jnp.where(valid, v, 0.0)
    w_log = -_softplus(-(w0_ref[...] + _dot(jnp.tanh(zwa).astype(BF16), w2_ref[...]))) - 0.5
    ld = -jnp.exp(w_log)
    if Tv < L:
        ld = jnp.where(valid, ld, 0.0)
    iclr = _sigmoid(a0_ref[...] + _dot(zwa.astype(BF16), a2_ref[...]))
    g = _dot(_sigmoid(zg).astype(BF16), g2_ref[...])
    kkraw = k * kk_ref[...]
    kmod = k * (1.0 + (iclr - 1.0) * ka_ref[...])

    ri = lax.broadcasted_iota(jnp.int32, (L, L), 0)
    ci = lax.broadcasted_iota(jnp.int32, (L, L), 1)
    strict = ri > ci
    incl = ri >= ci
    tri = jnp.where(incl, 1.0, 0.0).astype(BF16)
    ld_hi = ld.astype(BF16)
    rem = ld - ld_hi.astype(F32)
    ld_mid = rem.astype(BF16)
    ld_lo = (rem - ld_mid.astype(F32)).astype(BF16)
    cum = _dot(tri, ld_hi) + _dot(tri, ld_mid) + _dot(tri, ld_lo)
    cum_l = cum[L - 1:L, :]
    cumm = cum - cum[L // 2 - 1:L // 2, :]
    e_m = jnp.exp(cumm)
    e_mi = jnp.exp(-cumm)
    e_mprev = jnp.exp(cumm - ld)
    e_0 = jnp.exp(cum)
    e_0prev = jnp.exp(cum - ld)
    e_end = jnp.exp(cum_l - cum)
    p_l = jnp.exp(cum_l)
    lane = lax.broadcasted_iota(jnp.int32, (1, 2 * RWKV_HEAD), 1)
    m0 = lane < RWKV_HEAD
    bi = lax.broadcasted_iota(jnp.int32, (2 * RWKV_HEAD, 2 * RWKV_HEAD), 0) < RWKV_HEAD
    bj = lax.broadcasted_iota(jnp.int32, (2 * RWKV_HEAD, 2 * RWKV_HEAD), 1) < RWKV_HEAD
    blockdiag = bi == bj

    def headsum(x):
        s0 = jnp.sum(jnp.where(m0, x, 0.0), axis=-1, keepdims=True)
        s1 = jnp.sum(jnp.where(m0, 0.0, x), axis=-1, keepdims=True)
        return jnp.where(m0, s0, s1)

    P = range(RWKV_PAIRS)
    sls = [slice(p * 128, (p + 1) * 128) for p in P]
    pre = []
    for p in P:
        sl = sls[p]
        r_p, v_p, kmod_p, iclr_p = r[:, sl], v[:, sl], kmod[:, sl], iclr[:, sl]
        kkr = kkraw[:, sl]
        kk = kkr / jnp.maximum(jnp.sqrt(headsum(kkr * kkr)), 1e-12)
        a_p = -kk
        b_p = kk * iclr_p
        at = a_p * e_mprev[:, sl]
        rt = r_p * e_m[:, sl]
        bk = jnp.concatenate([b_p * e_mi[:, sl], kmod_p * e_mi[:, sl]], axis=0).astype(BF16)
        ar0 = jnp.concatenate([a_p * e_0prev[:, sl], r_p * e_0[:, sl]], axis=0).astype(BF16)
        be = jnp.concatenate([b_p * e_end[:, sl], kmod_p * e_end[:, sl]], axis=0).astype(BF16)
        pre.append(dict(r=r_p, v=v_p, kmod=kmod_p, at=at, rt=rt, bk=bk, ar0=ar0, be=be,
                        vb=v_p.astype(BF16)))
    wts = [wt_s[p] for p in P]
    as0 = [_dot_nt(pre[p]["ar0"], wts[p].astype(BF16)) for p in P]
    heads = [(p, h) for p in P for h in range(2)]
    gms = {}
    for (p, h) in heads:
        mh = m0 if h == 0 else jnp.logical_not(m0)
        arm = jnp.concatenate([jnp.where(mh, pre[p]["at"], 0.0), jnp.where(mh, pre[p]["rt"], 0.0)],
                              axis=0)
        gms[(p, h)] = _dot_nt(arm.astype(BF16), pre[p]["bk"])
    nab, mm, ys = {}, {}, {}
    for (p, h) in heads:
        gm = gms[(p, h)]
        nab[(p, h)] = jnp.where(strict, gm[0:L, 0:L], 0.0).astype(BF16)
        n_ak = jnp.where(strict, gm[0:L, L:2 * L], 0.0)
        mm[(p, h)] = jnp.concatenate([jnp.where(incl, gm[L:2 * L, 0:L], 0.0),
                                      jnp.where(incl, gm[L:2 * L, L:2 * L], 0.0)],
                                     axis=1).astype(BF16)
        ys[(p, h)] = as0[p][0:L] + _dot(n_ak.astype(BF16), pre[p]["vb"])
    pws = dict(nab)
    nlev = int(math.log2(L))
    for lev in range(nlev):
        for hd in heads:
            ys[hd] = ys[hd] + _dot(pws[hd], ys[hd].astype(BF16))
        if lev + 1 < nlev:
            for hd in heads:
                pws[hd] = _dot(pws[hd], pws[hd]).astype(BF16)
    for p in P:
        sl = sls[p]
        pp = pre[p]
        u_p = jnp.where(m0, ys[(p, 0)], ys[(p, 1)])
        uv = jnp.concatenate([u_p, pp["v"]], axis=0).astype(BF16)
        o0 = as0[p][L:2 * L] + _dot(mm[(p, 0)], uv)
        o1 = as0[p][L:2 * L] + _dot(mm[(p, 1)], uv)
        o = jnp.where(m0, o0, o1)
        wt_new = wts[p] * p_l[:, sl] + jnp.where(blockdiag, _dot_tn(uv, pp["be"]), 0.0)
        wt_s[p] = wt_new
        wt_ref[p] = wt_new
        mean = headsum(o) * (1.0 / RWKV_HEAD)
        d = o - mean
        var = headsum(d * d) * (1.0 / RWKV_HEAD)
        on = d * lax.rsqrt(var + GN_EPS) * lnw_ref[:, sl] + lnb_ref[:, sl]
        bonus = headsum(pp["r"] * pp["kmod"] * rk_ref[:, sl]) * pp["v"]
        y = (on + bonus) * g[:, sl]
        y_ref[:, sl] = y[0:Tv]


def _rwkv(zr, sh0, wt0, p, B, T):
    Tv = min(T, RWKV_CHUNK)
    kern = functools.partial(_rwkv_kernel, Tv=Tv)
    row = lambda n: pl.BlockSpec((1, n), lambda b, c: (0, 0))
    mat = lambda m, n: pl.BlockSpec((m, n), lambda b, c: (0, 0))
    st = pl.BlockSpec((None, RWKV_PAIRS, 128, 128), lambda b, c: (b, 0, 0, 0))
    return pl.pallas_call(
        kern,
        grid=(B, T // Tv),
        in_specs=[
            pl.BlockSpec((Tv, RWKV_COLS), lambda b, c: (c, b)),
            pl.BlockSpec((None, 1, RWKV_COLS), lambda b, c: (b, 0, 0)),
            st,
            row(RWKV_COLS), row(RWKV_WIDTH), mat(LORA_WA, RWKV_WIDTH), row(RWKV_WIDTH),
            mat(LORA_WA, RWKV_WIDTH), mat(G_LORA, RWKV_WIDTH),
            row(RWKV_WIDTH), row(RWKV_WIDTH), row(RWKV_WIDTH), row(RWKV_WIDTH), row(RWKV_WIDTH),
        ],
        out_specs=[
            pl.BlockSpec((Tv, RWKV_WIDTH), lambda b, c: (c, b)),
            st,
        ],
        out_shape=[
            jax.ShapeDtypeStruct((T, B * RWKV_WIDTH), F32),
            jax.ShapeDtypeStruct((B, RWKV_PAIRS, 128, 128), F32),
        ],
        scratch_shapes=[
            pltpu.VMEM((1, RWKV_COLS), F32),
            pltpu.VMEM((RWKV_PAIRS, 128, 128), F32),
        ],
        compiler_params=_cparams(("arbitrary", "arbitrary")),
        name="rwkv7",
    )(zr, sh0, wt0, p["mu"], p["w0"], p["w2"], p["a0"], p["a2"], p["g2"],
      p["kk"], p["ka"], p["rk"], p["lnw"], p["lnb"])


def _outproj_kernel(x_ref, yl_ref, yr_ref, wa_ref, wb_ref, o_ref):
    o_ref[...] = (x_ref[...] + _dot(yl_ref[...].astype(BF16), wa_ref[...])
                  + _dot(yr_ref[...].astype(BF16), wb_ref[...]))


def _outproj(x, yl, yr, wa, wb, tm):
    B, T, D = x.shape
    return pl.pallas_call(
        _outproj_kernel,
        grid=(B, T // tm),
        in_specs=[
            pl.BlockSpec((None, tm, D), lambda b, i: (b, i, 0)),
            pl.BlockSpec((tm, LRU_WIDTH), lambda b, i: (i, b)),
            pl.BlockSpec((tm, RWKV_WIDTH), lambda b, i: (i, b)),
            pl.BlockSpec((LRU_WIDTH, D), lambda b, i: (0, 0)),
            pl.BlockSpec((RWKV_WIDTH, D), lambda b, i: (0, 0)),
        ],
        out_specs=pl.BlockSpec((None, tm, D), lambda b, i: (b, i, 0)),
        out_shape=jax.ShapeDtypeStruct((B, T, D), F32),
        compiler_params=_cparams(("parallel", "parallel")),
        name="outproj",
    )(x, yl, yr, wa, wb)


def _sort16_pairs():
    pairs = []

    def merge(lo, n, r):
        step = 2 * r
        if step < n:
            merge(lo, n, step)
            merge(lo + r, n, step)
            for i in range(lo + r, lo + n - r, step):
                pairs.append((i, i + r))
        else:
            pairs.append((lo, lo + r))

    def sort(lo, n):
        if n > 1:
            sort(lo, n // 2)
            sort(lo + n // 2, n // 2)
            merge(lo, n, 1)

    sort(0, 16)
    return pairs


_SORT16 = _sort16_pairs()


def _merge_top(g, n):
    g = list(g)
    S = g[0].shape[0]
    sub = lax.broadcasted_iota(jnp.int32, g[0].shape, 0).astype(F32)
    rows = []
    for it in range(n):
        m = jnp.max(g[0], axis=0, keepdims=True)
        rows.append(m)
        if it + 1 < n:
            first = jnp.min(jnp.where(g[0] == m, sub, float(S)), axis=0, keepdims=True)
            sel = sub == first
            for v in range(min(n - 1 - it, len(g) - 1)):
                g[v] = jnp.where(sel, g[v + 1], g[v])
            if len(g) <= n - 1 - it:
                g[-1] = jnp.where(sel, -jnp.inf, g[-1])
    return rows


def _top16_rows(x):
    g = [x[8 * v:8 * v + 8, :] for v in range(16)]
    for (i, j) in _SORT16:
        g[i], g[j] = jnp.maximum(g[i], g[j]), jnp.minimum(g[i], g[j])
    return _merge_top(g, PEER_TOPK)


def _gelu2(x):
    c = math.sqrt(2.0 / math.pi)
    inner = x * (c + (c * 0.044715) * (x * x))
    hx = 0.5 * x
    return hx + hx * jnp.tanh(inner)


def _zero_like_dep(x):
    u = pltpu.bitcast(x, jnp.uint32)
    z = lax.shift_right_logical(lax.shift_right_logical(u, jnp.uint32(16)), jnp.uint32(16))
    return pltpu.bitcast(z, F32)


def _add_dep(a, z):
    zb = jnp.concatenate([z, z], axis=0).astype(BF16)
    return a + jnp.tile(zb, (a.shape[0] // 16, a.shape[1] // 128))


def _peer_kernel(h_ref, g_ref, wq_ref, keys_ref, u_ref, vt_ref, o_ref,
                 xnt_s, s1_s, s2_s, tau_s, s1x_s, w_s, acc_s, *, Tt, Ec):
    j = pl.program_id(1)
    nI = Ec // N_KEYS
    nT = Tt // 128
    Kk = PEER_TOPK
    log2e = 1.0 / math.log(2.0)

    @pl.when(j == 0)
    def _prep():
        xn = _rms(h_ref[...], g_ref[...])
        xnb = xn.astype(BF16)
        xnt_s[...] = xn.T.astype(BF16)
        acc_s[...] = jnp.zeros_like(acc_s)
        sub = lax.broadcasted_iota(jnp.int32, (Kk, Tt), 0)

        def cands(a1, t2):
            return [jnp.where(sub < Kk // (q + 1), a1 + t2[q], -jnp.inf) for q in range(Kk)]

        for h in range(PEER_HEADS):
            q = _dot_nt(wq_ref[h * 2 * PEER_HALF:(h + 1) * 2 * PEER_HALF, :], xnb)
            s1 = _dot(keys_ref[2 * h], q[0:PEER_HALF].astype(BF16))
            s2 = _dot(keys_ref[2 * h + 1], q[PEER_HALF:].astype(BF16))
            t1 = _top16_rows(s1)
            t2 = _top16_rows(s2)
            sc1 = (s1 - t1[0]) * log2e
            sc2 = (s2 - t2[0]) * log2e
            a1 = (jnp.concatenate(t1, axis=0) - t1[0]) * log2e
            a2 = [(t - t2[0]) * log2e for t in t2]
            zsum = jnp.zeros_like(t1[0])
            for b_ in _merge_top(cands(a1, a2), Kk):
                zsum = zsum + jnp.exp2(b_)
            lz = jnp.log2(zsum)
            sc2 = sc2 - lz
            a2 = [t - lz for t in a2]
            tau_s[h:h + 1, :] = _merge_top(cands(a1, a2), Kk)[Kk - 1]
            for tb in range(nT):
                cs = slice(tb * 128, (tb + 1) * 128)
                s1_s[h, tb] = sc1[:, cs]
                s2_s[h, tb] = sc2[:, cs]

    HK = N_KEYS // 2
    IG = PEER_IG
    GR = IG * N_KEYS
    r0d = pl.multiple_of(j * nI, nI)
    for h in range(PEER_HEADS):
        for tb in range(nT):
            s1x_s[h, tb] = s1_s[h, tb, pl.ds(r0d, nI), :]

    def hid_piece(ig, z):
        lhs = u_ref[ig * N_KEYS:ig * N_KEYS + GR, :]
        if z is not None:
            lhs = _add_dep(lhs, z)
        return _dot(lhs, xnt_s[...])

    groups = list(range(0, nI, IG))
    zc = jnp.zeros((8, 128), F32)
    hid = hid_piece(groups[0], None)
    for gi, ig in enumerate(groups):
        hid_next = None
        for tb in range(nT):
            cs = slice(tb * 128, (tb + 1) * 128)
            for half in range(2):
                ks = slice(half * HK, (half + 1) * HK)
                accs = [jnp.concatenate([zc] * (HK // 8), axis=0) for _ in range(IG)]
                for h in range(PEER_HEADS):
                    s2h = s2_s[h, tb, ks, :]
                    tau = tau_s[h:h + 1, cs]
                    for ii in range(IG):
                        i1 = ig + ii
                        cval = s1x_s[h, tb, i1:i1 + 1, :] + s2h
                        accs[ii] = accs[ii] + jnp.where(cval >= tau, jnp.exp2(cval), 0.0)
                zc = _zero_like_dep(accs[0][0:8, :])
                if tb == 0 and half == 0 and gi + 1 < len(groups):
                    hid_next = hid_piece(groups[gi + 1], zc)
                for ii in range(IG):
                    r0 = (ig + ii) * N_KEYS + half * HK
                    hr = slice(ii * N_KEYS + half * HK, ii * N_KEYS + (half + 1) * HK)
                    w_s[r0:r0 + HK, cs] = accs[ii].astype(BF16) * _gelu2(hid[hr, cs].astype(BF16))
        if gi % 2 == 1:
            kr = slice((ig - IG) * N_KEYS, (ig + IG) * N_KEYS)
            acc_s[...] += _dot(vt_ref[:, kr], w_s[kr, :])
        hid = hid_next

    @pl.when(j == pl.num_programs(1) - 1)
    def _fin():
        o_ref[...] = h_ref[...] + acc_s[...].T


def _peer(h, g, wq_t, keys, u, vt, Tt, Ec):
    N, D = h.shape
    nT = Tt // 128
    nI = Ec // N_KEYS
    kern = functools.partial(_peer_kernel, Tt=Tt, Ec=Ec)
    score = pltpu.VMEM((PEER_HEADS, nT, N_KEYS, 128), F32)
    rows = pltpu.VMEM((PEER_HEADS, nT, nI, 128), F32)
    return pl.pallas_call(
        kern,
        grid=(N // Tt, N_EXPERTS // Ec),
        in_specs=[
            pl.BlockSpec((Tt, D), lambda i, j: (i, 0)),
            pl.BlockSpec((1, D), lambda i, j: (0, 0)),
            pl.BlockSpec((PEER_HEADS * 2 * PEER_HALF, D), lambda i, j: (0, 0)),
            pl.BlockSpec((2 * PEER_HEADS, N_KEYS, PEER_HALF), lambda i, j: (0, 0, 0)),
            pl.BlockSpec((Ec, D), lambda i, j: (j, 0)),
            pl.BlockSpec((None, D, Ec), lambda i, j: (j, 0, 0)),
        ],
        out_specs=pl.BlockSpec((Tt, D), lambda i, j: (i, 0)),
        out_shape=jax.ShapeDtypeStruct((N, D), F32),
        scratch_shapes=[
            pltpu.VMEM((D, Tt), BF16),
            score, score,
            pltpu.VMEM((PEER_HEADS, Tt), F32),
            rows,
            pltpu.VMEM((Ec, Tt), BF16),
            pltpu.VMEM((D, Tt), F32),
        ],
        compiler_params=_cparams(("parallel", "arbitrary")),
        name="peer",
    )(h, g, wq_t, keys, u, vt)


def _ple_kernel(h_ref, p_ref, g_ref, pw_ref, gw_ref, fg_ref, o_ref, *, final):
    h = h_ref[...]
    gate = _sigmoid(_dot(_rms(h, g_ref[...]).astype(BF16), gw_ref[...]))
    out = h + _dot(p_ref[...].astype(BF16), pw_ref[...]) * gate
    if final:
        out = _rms(out, fg_ref[...])
    o_ref[...] = out


def _ple(h, p, g, pw, gw, fg, tm, final):
    N, D = h.shape
    return pl.pallas_call(
        functools.partial(_ple_kernel, final=final),
        grid=(N // tm,),
        in_specs=[
            pl.BlockSpec((tm, D), lambda i: (i, 0)),
            pl.BlockSpec((tm, D_PLE), lambda i: (i, 0)),
            pl.BlockSpec((1, D), lambda i: (0, 0)),
            pl.BlockSpec((D_PLE, D), lambda i: (0, 0)),
            pl.BlockSpec((D, D), lambda i: (0, 0)),
            pl.BlockSpec((1, D), lambda i: (0, 0)),
        ],
        out_specs=pl.BlockSpec((tm, D), lambda i: (i, 0)),
        out_shape=jax.ShapeDtypeStruct((N, D), F32),
        compiler_params=_cparams(("parallel",)),
        name="ple",
    )(h, p, g, pw, gw, fg)


def _layer_params(i, norm1_g, w_in, conv_w, conv_b, lru_wa, lru_ba, lru_wx, lru_bx, lru_lambda,
                  rwkv_mu, rwkv_w0, rwkv_w2, rwkv_a0, rwkv_a2, rwkv_g2, rwkv_kk, rwkv_ka, rwkv_rk,
                  rwkv_ln_w, rwkv_ln_b, w_out, norm2_g, peer_wq, peer_keys, peer_u, peer_v,
                  norm3_g, ple_w, ple_gate):
    row = lambda a: a.reshape(1, -1)
    wi = w_in[i].astype(BF16)
    zpad = jnp.zeros((LORA_WA // 2, RWKV_WIDTH), BF16)
    return dict(
        g1=row(norm1_g[i]), wl=wi[:, :LRU_COLS], wr=wi[:, LRU_COLS:],
        lru=dict(cw=conv_w[i], cb=row(conv_b[i]), wa=lru_wa[i].astype(BF16), ba=row(lru_ba[i]),
                 wx=lru_wx[i].astype(BF16), bx=row(lru_bx[i]), lam=row(lru_lambda[i])),
        rwkv=dict(mu=row(rwkv_mu[i]), w0=row(rwkv_w0[i]),
                  w2=jnp.concatenate([rwkv_w2[i].astype(BF16), zpad], axis=0), a0=row(rwkv_a0[i]),
                  a2=jnp.concatenate([zpad, rwkv_a2[i].astype(BF16)], axis=0),
                  g2=rwkv_g2[i].astype(BF16), kk=row(rwkv_kk[i]), ka=row(rwkv_ka[i]),
                  rk=row(rwkv_rk[i]), lnw=row(rwkv_ln_w[i]), lnb=row(rwkv_ln_b[i])),
        wo_a=w_out[i, :LRU_WIDTH].astype(BF16), wo_b=w_out[i, LRU_WIDTH:].astype(BF16),
        g2=row(norm2_g[i]), wq_t=peer_wq[i].T.astype(BF16),
        keys=peer_keys[i].reshape(2 * PEER_HEADS, N_KEYS, PEER_HALF).astype(BF16),
        u=peer_u[i].astype(BF16),
        vt=jnp.swapaxes(peer_v[i].astype(BF16).reshape(N_EXPERTS // PEER_EC, PEER_EC, D_MODEL), 1, 2),
        g3=row(norm3_g[i]), pw=ple_w[i].astype(BF16), gw=ple_gate[i].astype(BF16),
    )


def _pairs_from_state(s):
    B = s.shape[0]
    s = s.reshape(B, RWKV_PAIRS, 2, RWKV_HEAD, RWKV_HEAD)
    z = jnp.zeros_like(s[:, :, 0])
    top = jnp.concatenate([s[:, :, 0], z], axis=-1)
    bot = jnp.concatenate([z, s[:, :, 1]], axis=-1)
    return jnp.concatenate([top, bot], axis=-2)


def _state_from_pairs(w):
    B = w.shape[0]
    w = w.reshape(B, RWKV_PAIRS, 2, RWKV_HEAD, 2, RWKV_HEAD)
    return jnp.stack([w[:, :, 0, :, 0, :], w[:, :, 1, :, 1, :]], axis=2).reshape(
        B, RWKV_HEADS, RWKV_HEAD, RWKV_HEAD)


def _trunk(x, p, conv_st, lru_st, shift_st, wkv_st, first, layers, final_g, tiles):
    B, T, D = x.shape
    depth = len(layers)
    tm, lt = tiles
    tt, ec = PEER_TT, PEER_EC
    h = x
    convs, lrus, shifts, wkvs = [], [], [], []
    for i, lp in enumerate(layers):
        zl, zr = _inproj(h, lp["g1"], lp["wl"], lp["wr"], tm)
        conv0 = jnp.swapaxes(conv_st[i], 0, 1).reshape((CONV_W - 1) * B, LRU_WIDTH)
        y_lru, h_lru = _lru(zl.reshape(T * B, LRU_COLS), conv0, lru_st[i], lp["lru"], B, T, lt, first)
        y_rw, wt = _rwkv(zr, shift_st[i].reshape(B, 1, RWKV_COLS), _pairs_from_state(wkv_st[i]),
                         lp["rwkv"], B, T)
        h1 = _outproj(h, y_lru.reshape(T, B * LRU_WIDTH), y_rw, lp["wo_a"], lp["wo_b"], tm)
        h2 = _peer(h1.reshape(B * T, D), lp["g2"], lp["wq_t"], lp["keys"], lp["u"], lp["vt"], tt, ec)
        h3 = _ple(h2, p[i].reshape(B * T, D_PLE), lp["g3"], lp["pw"], lp["gw"], final_g, tt,
                  final=(i == depth - 1))
        h = h3.reshape(B, T, D)
        zl3 = zl.reshape(T, B, LRU_COLS)
        convs.append(jnp.swapaxes(zl3[T - (CONV_W - 1):, :, :LRU_WIDTH], 0, 1))
        lrus.append(h_lru)
        shifts.append(zr.reshape(T, B, RWKV_COLS)[T - 1])
        wkvs.append(_state_from_pairs(wt))
    return h, jnp.stack(convs), jnp.stack(lrus), jnp.stack(shifts), jnp.stack(wkvs)


def kernel(x_prompt, x_sample, p_prompt, p_sample, state_conv, state_lru, state_shift, state_wkv, norm1_g, w_in, conv_w, conv_b, lru_wa, lru_ba, lru_wx, lru_bx, lru_lambda, rwkv_mu, rwkv_w0, rwkv_w2, rwkv_a0, rwkv_a2, rwkv_g2, rwkv_kk, rwkv_ka, rwkv_rk, rwkv_ln_w, rwkv_ln_b, w_out, norm2_g, peer_wq, peer_keys, peer_u, peer_v, norm3_g, ple_w, ple_gate, final_g):
    depth = w_in.shape[0]
    layers = [_layer_params(i, norm1_g, w_in, conv_w, conv_b, lru_wa, lru_ba, lru_wx, lru_bx,
                            lru_lambda, rwkv_mu, rwkv_w0, rwkv_w2, rwkv_a0, rwkv_a2, rwkv_g2,
                            rwkv_kk, rwkv_ka, rwkv_rk, rwkv_ln_w, rwkv_ln_b, w_out, norm2_g,
                            peer_wq, peer_keys, peer_u, peer_v, norm3_g, ple_w, ple_gate)
              for i in range(depth)]
    fg = final_g.reshape(1, -1)
    bp = x_prompt.shape[0]
    dt = x_prompt.dtype
    zc = jnp.zeros((depth, bp, CONV_W - 1, LRU_WIDTH), dt)
    zl = jnp.zeros((depth, bp, LRU_WIDTH), dt)
    zs = jnp.zeros((depth, bp, RWKV_COLS), dt)
    zw = jnp.zeros((depth, bp, RWKV_HEADS, RWKV_HEAD, RWKV_HEAD), dt)
    tp = min(512, x_prompt.shape[1])
    ts = min(512, x_sample.shape[1])
    out_p = _trunk(x_prompt, p_prompt, zc, zl, zs, zw, True, layers, fg,
                   (tp, min(32, x_prompt.shape[1])))
    out_s = _trunk(x_sample, p_sample, state_conv, state_lru, state_shift, state_wkv, False,
                   layers, fg, (ts, min(32, x_sample.shape[1])))
    return (out_p[0], out_s[0]) + tuple(out_p[1:]) + tuple(out_s[1:])
```

```python
import functools
import math

import jax
import jax.numpy as jnp
from jax import lax
from jax.experimental import pallas as pl
from jax.experimental.pallas import tpu as pltpu

F32 = jnp.float32
BF16 = jnp.bfloat16

D_MODEL = 1024
D_PLE = 256
LRU_WIDTH = 512
LRU_BLOCKS = 4
LRU_BLOCK = 128
CONV_W = 4
LRU_C = 8.0
RWKV_WIDTH = 512
RWKV_HEAD = 64
RWKV_HEADS = 8
RWKV_PAIRS = RWKV_HEADS // 2
LORA_WA = 128
G_LORA = 128
RWKV_COLS = 3 * RWKV_WIDTH + LORA_WA + G_LORA
LRU_COLS = 2 * LRU_WIDTH
PEER_HEADS = 8
N_KEYS = 128
N_EXPERTS = N_KEYS * N_KEYS
PEER_HALF = 128
PEER_TOPK = 16
PEER_IG = 2
PEER_TT = 512
PEER_EC = 1024
EPS = 1e-6
GN_EPS = 64e-5

RWKV_CHUNK = 128
VMEM_LIMIT = 56 * 1024 * 1024


def _cparams(sem):
    return pltpu.CompilerParams(dimension_semantics=sem, vmem_limit_bytes=VMEM_LIMIT)


def _rms(x, g):
    return x * lax.rsqrt(jnp.mean(x * x, axis=-1, keepdims=True) + EPS) * g


def _gelu(x):
    c = math.sqrt(2.0 / math.pi)
    return x * (0.5 * (1.0 + jnp.tanh(c * (x + 0.044715 * (x * x * x)))))


def _sigmoid(x):
    return 1.0 / (1.0 + jnp.exp(-x))


def _softplus(x):
    return jnp.maximum(x, 0.0) + jnp.log1p(jnp.exp(-jnp.abs(x)))


def _dot(a, b):
    return jnp.dot(a, b, preferred_element_type=F32)


def _dot_nt(a, b):
    return lax.dot_general(a, b, (((1,), (1,)), ((), ())), preferred_element_type=F32)


def _dot_tn(a, b):
    return lax.dot_general(a, b, (((0,), (0,)), ((), ())), preferred_element_type=F32)


def _inproj_kernel(x_ref, g_ref, wl_ref, wr_ref, zl_ref, zr_ref):
    xn = _rms(x_ref[...], g_ref[...]).astype(BF16)
    zl_ref[...] = _dot(xn, wl_ref[...])
    zr_ref[...] = _dot(xn, wr_ref[...])


def _inproj(x, g, wl, wr, tm):
    B, T, D = x.shape
    return pl.pallas_call(
        _inproj_kernel,
        grid=(B, T // tm),
        in_specs=[
            pl.BlockSpec((None, tm, D), lambda b, i: (b, i, 0)),
            pl.BlockSpec((1, D), lambda b, i: (0, 0)),
            pl.BlockSpec((D, LRU_COLS), lambda b, i: (0, 0)),
            pl.BlockSpec((D, RWKV_COLS), lambda b, i: (0, 0)),
        ],
        out_specs=[
            pl.BlockSpec((tm, LRU_COLS), lambda b, i: (i, b)),
            pl.BlockSpec((tm, RWKV_COLS), lambda b, i: (i, b)),
        ],
        out_shape=[
            jax.ShapeDtypeStruct((T, B * LRU_COLS), F32),
            jax.ShapeDtypeStruct((T, B * RWKV_COLS), F32),
        ],
        compiler_params=_cparams(("parallel", "parallel")),
        name="inproj",
    )(x, g, wl, wr)


def _lru_kernel(z_ref, conv0_ref, h0_ref, cw_ref, cb_ref, wa_ref, ba_ref, wx_ref, bx_ref,
                lam_ref, y_ref, hlast_ref, hist_s, h_s, a_s, b_s, *, B, Lt, first):
    i = pl.program_id(0)
    W = LRU_WIDTH
    R = Lt * B

    @pl.when(i == 0)
    def _():
        hist_s[...] = conv0_ref[...]
        h_s[...] = h0_ref[...]

    xb = z_ref[:, 0:W]
    gb = z_ref[:, W:2 * W]
    full = jnp.concatenate([hist_s[...], xb], axis=0)
    xc = cb_ref[...] + full[0:R] * cw_ref[0:1, :]
    for j in range(1, CONV_W):
        xc = xc + full[j * B:j * B + R] * cw_ref[j:j + 1, :]
    hist_s[...] = full[R:R + (CONV_W - 1) * B]

    ra, ix = [], []
    for n in range(LRU_BLOCKS):
        xn = xc[:, n * LRU_BLOCK:(n + 1) * LRU_BLOCK].astype(BF16)
        ra.append(_dot(xn, wa_ref[n]))
        ix.append(_dot(xn, wx_ref[n]))
    r = _sigmoid(jnp.concatenate(ra, axis=1) + ba_ref[...])
    ig = _sigmoid(jnp.concatenate(ix, axis=1) + bx_ref[...])
    log_a = -LRU_C * r * _softplus(-lam_ref[...])
    a = jnp.exp(log_a)
    t = jnp.tanh(log_a)
    mult = jnp.sqrt(-2.0 * t / (1.0 - t))
    if first:
        row = lax.broadcasted_iota(jnp.int32, (R, 1), 0)
        mult = jnp.where(row < jnp.where(i == 0, B, 0), 1.0, mult)
    a_s[...] = a
    b_s[...] = mult * (ig * xc)

    def step(tt, h):
        r0 = pl.multiple_of(tt * B, B)
        h = a_s[pl.ds(r0, B), :] * h + b_s[pl.ds(r0, B), :]
        b_s[pl.ds(r0, B), :] = h
        return h

    h = lax.fori_loop(0, Lt, step, h_s[...], unroll=8)
    h_s[...] = h
    hlast_ref[...] = h
    y_ref[...] = b_s[...] * _gelu(gb)


def _lru(zl, conv0, h0, p, B, T, Lt, first):
    W = LRU_WIDTH
    R = Lt * B
    kern = functools.partial(_lru_kernel, B=B, Lt=Lt, first=first)
    full = lambda shape: pl.BlockSpec(shape, lambda i: (0,) * len(shape))
    return pl.pallas_call(
        kern,
        grid=(T // Lt,),
        in_specs=[
            pl.BlockSpec((R, LRU_COLS), lambda i: (i, 0)),
            full(((CONV_W - 1) * B, W)),
            full((B, W)),
            full((CONV_W, W)),
            full((1, W)),
            full((LRU_BLOCKS, LRU_BLOCK, LRU_BLOCK)),
            full((1, W)),
            full((LRU_BLOCKS, LRU_BLOCK, LRU_BLOCK)),
            full((1, W)),
            full((1, W)),
        ],
        out_specs=[
            pl.BlockSpec((R, W), lambda i: (i, 0)),
            full((B, W)),
        ],
        out_shape=[
            jax.ShapeDtypeStruct((T * B, W), F32),
            jax.ShapeDtypeStruct((B, W), F32),
        ],
        scratch_shapes=[
            pltpu.VMEM(((CONV_W - 1) * B, W), F32),
            pltpu.VMEM((B, W), F32),
            pltpu.VMEM((R, W), F32),
            pltpu.VMEM((R, W), F32),
        ],
        compiler_params=_cparams(("arbitrary",)),
        name="rglru",
    )(zl, conv0, h0, p["cw"], p["cb"], p["wa"], p["ba"], p["wx"], p["bx"], p["lam"])


def _rwkv_kernel(z_ref, sh0_ref, wt0_ref, mu_ref, w0_ref, w2_ref, a0_ref, a2_ref, g2_ref,
                 kk_ref, ka_ref, rk_ref, lnw_ref, lnb_ref, y_ref, wt_ref, zprev_s, wt_s, *, Tv):
    L = RWKV_CHUNK
    Wd = RWKV_WIDTH
    c = pl.program_id(1)

    @pl.when(c == 0)
    def _():
        zprev_s[...] = sh0_ref[...]
        wt_s[...] = wt0_ref[...]

    z = z_ref[...]
    if Tv < L:
        z = jnp.concatenate([z, jnp.zeros((L - Tv, RWKV_COLS), F32)], axis=0)
    row = lax.broadcasted_iota(jnp.int32, (L, 1), 0)
    zp = jnp.where(row == 0, zprev_s[...], pltpu.roll(z, 1, 0))
    zprev_s[...] = z[Tv - 1:Tv, :]
    zs = z + (zp - z) * mu_ref[...]
    r = zs[:, 0:Wd]
    k = zs[:, Wd:2 * Wd]
    v = zs[:, 2 * Wd:3 * Wd]
    zwa = zs[:, 3 * Wd:3 * Wd + LORA_WA]
    zg = zs[:, 3 * Wd + LORA_WA:]
    if Tv < L:
        valid = row < Tv
        r = jnp.where(valid, r, 0.0)
        k = jnp.where(valid, k, 0.0)
        v = jnp.where(valid, v, 0.0)
    w_log = -_softplus(-(w0_ref[...] + _dot(jnp.tanh(zwa).astype(BF16), w2_ref[...]))) - 0.5
    ld = -jnp.exp(w_log)
    if Tv < L:
        ld = jnp.where(valid, ld, 0.0)
    iclr = _sigmoid(a0_ref[...] + _dot(zwa.astype(BF16), a2_ref[...]))
    g = _dot(_sigmoid(zg).astype(BF16), g2_ref[...])
    kkraw = k * kk_ref[...]
    kmod = k * (1.0 + (iclr - 1.0) * ka_ref[...])

    ri = lax.broadcasted_iota(jnp.int32, (L, L), 0)
    ci = lax.broadcasted_iota(jnp.int32, (L, L), 1)
    strict = ri > ci
    incl = ri >= ci
    tri = jnp.where(incl, 1.0, 0.0).astype(BF16)
    ld_hi = ld.astype(BF16)
    rem = ld - ld_hi.astype(F32)
    ld_mid = rem.astype(BF16)
    ld_lo = (rem - ld_mid.astype(F32)).astype(BF16)
    cum = _dot(tri, ld_hi) + _dot(tri, ld_mid) + _dot(tri, ld_lo)
    cum_l = cum[L - 1:L, :]
    cumm = cum - cum[L // 2 - 1:L // 2, :]
    e_m = jnp.exp(cumm)
    e_mi = jnp.exp(-cumm)
    e_mprev = jnp.exp(cumm - ld)
    e_0 = jnp.exp(cum)
    e_0prev = jnp.exp(cum - ld)
    e_end = jnp.exp(cum_l - cum)
    p_l = jnp.exp(cum_l)
    lane = lax.broadcasted_iota(jnp.int32, (1, 2 * RWKV_HEAD), 1)
    m0 = lane < RWKV_HEAD
    bi = lax.broadcasted_iota(jnp.int32, (2 * RWKV_HEAD, 2 * RWKV_HEAD), 0) < RWKV_HEAD
    bj = lax.broadcasted_iota(jnp.int32, (2 * RWKV_HEAD, 2 * RWKV_HEAD), 1) < RWKV_HEAD
    blockdiag = bi == bj

    def headsum(x):
        s0 = jnp.sum(jnp.where(m0, x, 0.0), axis=-1, keepdims=True)
        s1 = jnp.sum(jnp.where(m0, 0.0, x), axis=-1, keepdims=True)
        return jnp.where(m0, s0, s1)

    P = range(RWKV_PAIRS)
    sls = [slice(p * 128, (p + 1) * 128) for p in P]
    pre = []
    for p in P:
        sl = sls[p]
        r_p, v_p, kmod_p, iclr_p = r[:, sl], v[:, sl], kmod[:, sl], iclr[:, sl]
        kkr = kkraw[:, sl]
        kk = kkr / jnp.maximum(jnp.sqrt(headsum(kkr * kkr)), 1e-12)
        a_p = -kk
        b_p = kk * iclr_p
        at = a_p * e_mprev[:, sl]
        rt = r_p * e_m[:, sl]
        bk = jnp.concatenate([b_p * e_mi[:, sl], kmod_p * e_mi[:, sl]], axis=0).astype(BF16)
        ar0 = jnp.concatenate([a_p * e_0prev[:, sl], r_p * e_0[:, sl]], axis=0).astype(BF16)
        be = jnp.concatenate([b_p * e_end[:, sl], kmod_p * e_end[:, sl]], axis=0).astype(BF16)
        pre.append(dict(r=r_p, v=v_p, kmod=kmod_p, at=at, rt=rt, bk=bk, ar0=ar0, be=be,
                        vb=v_p.astype(BF16)))
    wts = [wt_s[p] for p in P]
    as0 = [_dot_nt(pre[p]["ar0"], wts[p].astype(BF16)) for p in P]
    heads = [(p, h) for p in P for h in range(2)]
    gms = {}
    for (p, h) in heads:
        mh = m0 if h == 0 else jnp.logical_not(m0)
        arm = jnp.concatenate([jnp.where(mh, pre[p]["at"], 0.0), jnp.where(mh, pre[p]["rt"], 0.0)],
                              axis=0)
        gms[(p, h)] = _dot_nt(arm.astype(BF16), pre[p]["bk"])
    nab, mm, ys = {}, {}, {}
    for (p, h) in heads:
        gm = gms[(p, h)]
        nab[(p, h)] = jnp.where(strict, gm[0:L, 0:L], 0.0).astype(BF16)
        n_ak = jnp.where(strict, gm[0:L, L:2 * L], 0.0)
        mm[(p, h)] = jnp.concatenate([jnp.where(incl, gm[L:2 * L, 0:L], 0.0),
                                      jnp.where(incl, gm[L:2 * L, L:2 * L], 0.0)],
                                     axis=1).astype(BF16)
        ys[(p, h)] = as0[p][0:L] + _dot(n_ak.astype(BF16), pre[p]["vb"])
    pws = dict(nab)
    nlev = int(math.log2(L))
    for lev in range(nlev):
        for hd in heads:
            ys[hd] = ys[hd] + _dot(pws[hd], ys[hd].astype(BF16))
        if lev + 1 < nlev:
            for hd in heads:
                pws[hd] = _dot(pws[hd], pws[hd]).astype(BF16)
    for p in P:
        sl = sls[p]
        pp = pre[p]
        u_p = jnp.where(m0, ys[(p, 0)], ys[(p, 1)])
        uv = jnp.concatenate([u_p, pp["v"]], axis=0).astype(BF16)
        o0 = as0[p][L:2 * L] + _dot(mm[(p, 0)], uv)
        o1 = as0[p][L:2 * L] + _dot(mm[(p, 1)], uv)
        o = jnp.where(m0, o0, o1)
        wt_new = wts[p] * p_l[:, sl] + jnp.where(blockdiag, _dot_tn(uv, pp["be"]), 0.0)
        wt_s[p] = wt_new
        wt_ref[p] = wt_new
        mean = headsum(o) * (1.0 / RWKV_HEAD)
        d = o - mean
        var = headsum(d * d) * (1.0 / RWKV_HEAD)
        on = d * lax.rsqrt(var + GN_EPS) * lnw_ref[:, sl] + lnb_ref[:, sl]
        bonus = headsum(pp["r"] * pp["kmod"] * rk_ref[:, sl]) * pp["v"]
        y = (on + bonus) * g[:, sl]
        y_ref[:, sl] = y[0:Tv]


def _rwkv(zr, sh0, wt0, p, B, T):
    Tv = min(T, RWKV_CHUNK)
    kern = functools.partial(_rwkv_kernel, Tv=Tv)
    row = lambda n: pl.BlockSpec((1, n), lambda b, c: (0, 0))
    mat = lambda m, n: pl.BlockSpec((m, n), lambda b, c: (0, 0))
    st = pl.BlockSpec((None, RWKV_PAIRS, 128, 128), lambda b, c: (b, 0, 0, 0))
    return pl.pallas_call(
        kern,
        grid=(B, T // Tv),
        in_specs=[
            pl.BlockSpec((Tv, RWKV_COLS), lambda b, c: (c, b)),
            pl.BlockSpec((None, 1, RWKV_COLS), lambda b, c: (b, 0, 0)),
            st,
            row(RWKV_COLS), row(RWKV_WIDTH), mat(LORA_WA, RWKV_WIDTH), row(RWKV_WIDTH),
            mat(LORA_WA, RWKV_WIDTH), mat(G_LORA, RWKV_WIDTH),
            row(RWKV_WIDTH), row(RWKV_WIDTH), row(RWKV_WIDTH), row(RWKV_WIDTH), row(RWKV_WIDTH),
        ],
        out_specs=[
            pl.BlockSpec((Tv, RWKV_WIDTH), lambda b, c: (c, b)),
            st,
        ],
        out_shape=[
            jax.ShapeDtypeStruct((T, B * RWKV_WIDTH), F32),
            jax.ShapeDtypeStruct((B, RWKV_PAIRS, 128, 128), F32),
        ],
        scratch_shapes=[
            pltpu.VMEM((1, RWKV_COLS), F32),
            pltpu.VMEM((RWKV_PAIRS, 128, 128), F32),
        ],
        compiler_params=_cparams(("arbitrary", "arbitrary")),
        name="rwkv7",
    )(zr, sh0, wt0, p["mu"], p["w0"], p["w2"], p["a0"], p["a2"], p["g2"],
      p["kk"], p["ka"], p["rk"], p["lnw"], p["lnb"])


def _outproj_kernel(x_ref, yl_ref, yr_ref, wa_ref, wb_ref, o_ref):
    o_ref[...] = (x_ref[...] + _dot(yl_ref[...].astype(BF16), wa_ref[...])
                  + _dot(yr_ref[...].astype(BF16), wb_ref[...]))


def _outproj(x, yl, yr, wa, wb, tm):
    B, T, D = x.shape
    return pl.pallas_call(
        _outproj_kernel,
        grid=(B, T // tm),
        in_specs=[
            pl.BlockSpec((None, tm, D), lambda b, i: (b, i, 0)),
            pl.BlockSpec((tm, LRU_WIDTH), lambda b, i: (i, b)),
            pl.BlockSpec((tm, RWKV_WIDTH), lambda b, i: (i, b)),
            pl.BlockSpec((LRU_WIDTH, D), lambda b, i: (0, 0)),
            pl.BlockSpec((RWKV_WIDTH, D), lambda b, i: (0, 0)),
        ],
        out_specs=pl.BlockSpec((None, tm, D), lambda b, i: (b, i, 0)),
        out_shape=jax.ShapeDtypeStruct((B, T, D), F32),
        compiler_params=_cparams(("parallel", "parallel")),
        name="outproj",
    )(x, yl, yr, wa, wb)


def _sort16_pairs():
    pairs = []

    def merge(lo, n, r):
        step = 2 * r
        if step < n:
            merge(lo, n, step)
            merge(lo + r, n, step)
            for i in range(lo + r, lo + n - r, step):
                pairs.append((i, i + r))
        else:
            pairs.append((lo, lo + r))

    def sort(lo, n):
        if n > 1:
            sort(lo, n // 2)
            sort(lo + n // 2, n // 2)
            merge(lo, n, 1)

    sort(0, 16)
    return pairs


_SORT16 = _sort16_pairs()


def _merge_top(g, n, count=False):
    g = list(g)
    S = g[0].shape[0]
    sub = lax.broadcasted_iota(jnp.int32, g[0].shape, 0).astype(F32)
    rows = []
    cnt = jnp.zeros(g[0].shape, F32)
    for it in range(n):
        m = jnp.max(g[0], axis=0, keepdims=True)
        rows.append(m)
        if it + 1 < n or count:
            first = jnp.min(jnp.where(g[0] == m, sub, float(S)), axis=0, keepdims=True)
            sel = sub == first
            if count:
                cnt = cnt + jnp.where(sel, 1.0, 0.0)
        if it + 1 < n:
            for v in range(min(n - 1 - it, len(g) - 1)):
                g[v] = jnp.where(sel, g[v + 1], g[v])
            if len(g) <= n - 1 - it:
                g[-1] = jnp.where(sel, -jnp.inf, g[-1])
    return (rows, cnt) if count else rows


def _top16_rows(x):
    g = [x[8 * v:8 * v + 8, :] for v in range(16)]
    for (i, j) in _SORT16:
        g[i], g[j] = jnp.maximum(g[i], g[j]), jnp.minimum(g[i], g[j])
    return _merge_top(g, PEER_TOPK)


def _gelu2(x):
    c = math.sqrt(2.0 / math.pi)
    inner = x * (c + (c * 0.044715) * (x * x))
    hx = 0.5 * x
    return hx + hx * jnp.tanh(inner)


def _peer_kernel(h_ref, g_ref, wq_ref, keys_ref, u_ref, vt_ref, o_ref,
                 xnt_s, th_s, a_s, s2_s, b_s, thx_s, ax_s, w_s, hid_s, acc_s, *, Tt, Ec):
    j = pl.program_id(1)
    nI = Ec // N_KEYS
    nT = Tt // 128
    Kk = PEER_TOPK
    log2e = 1.0 / math.log(2.0)

    @pl.when(j == 0)
    def _prep():
        xn = _rms(h_ref[...], g_ref[...])
        xnb = xn.astype(BF16)
        xnt_s[...] = xn.T.astype(BF16)
        acc_s[...] = jnp.zeros_like(acc_s)
        sub = lax.broadcasted_iota(jnp.int32, (Kk, Tt), 0)
        for h in range(PEER_HEADS):
            q = _dot_nt(wq_ref[h * 2 * PEER_HALF:(h + 1) * 2 * PEER_HALF, :], xnb)
            s1 = _dot(keys_ref[2 * h], q[0:PEER_HALF].astype(BF16))
            s2 = _dot(keys_ref[2 * h + 1], q[PEER_HALF:].astype(BF16))
            t1 = _top16_rows(s1)
            t2 = _top16_rows(s2)
            a1 = (jnp.concatenate(t1, axis=0) - t1[0]) * log2e
            a2 = [(t - t2[0]) * log2e for t in t2]
            cand = [jnp.where(sub < Kk // (qq + 1), a1 + a2[qq], -jnp.inf) for qq in range(Kk)]
            best, n_p = _merge_top(cand, Kk, count=True)
            zsum = jnp.zeros_like(t1[0])
            for b_ in best:
                zsum = zsum + jnp.exp2(b_)
            th_p = jnp.full((Kk, Tt), jnp.inf, F32)
            for qq in range(Kk):
                th_p = jnp.where(n_p == float(qq + 1), t2[qq], th_p)
            theta = jnp.full((N_KEYS, Tt), jnp.inf, F32)
            for p in range(Kk):
                theta = jnp.where(s1 == t1[p], th_p[p:p + 1, :], theta)
            ga = jnp.exp2((s1 - t1[0]) * log2e) / zsum
            gb = jnp.exp2((s2 - t2[0]) * log2e)
            for tb in range(nT):
                cs = slice(tb * 128, (tb + 1) * 128)
                th_s[h, tb] = theta[:, cs]
                a_s[h, tb] = ga[:, cs]
                s2_s[h, tb] = s2[:, cs]
                b_s[h, tb] = gb[:, cs]

    HK = N_KEYS // 2
    IG = PEER_IG
    r0d = pl.multiple_of(j * nI, nI)
    for h in range(PEER_HEADS):
        for tb in range(nT):
            thx_s[h, tb] = th_s[h, tb, pl.ds(r0d, nI), :]
            ax_s[h, tb] = a_s[h, tb, pl.ds(r0d, nI), :]
    hid_s[...] = _dot(u_ref[...], xnt_s[...]).astype(BF16)
    for tb in range(nT):
        cs = slice(tb * 128, (tb + 1) * 128)
        for half in range(2):
            ks = slice(half * HK, (half + 1) * HK)
            for ig in range(0, nI, IG):
                accs = [jnp.zeros((HK, 128), F32) for _ in range(IG)]
                for h in range(PEER_HEADS):
                    s2h = s2_s[h, tb, ks, :]
                    bh = b_s[h, tb, ks, :]
                    for ii in range(IG):
                        i1 = ig + ii
                        gate = ax_s[h, tb, i1:i1 + 1, :] * bh
                        accs[ii] = accs[ii] + jnp.where(s2h >= thx_s[h, tb, i1:i1 + 1, :], gate, 0.0)
                for ii in range(IG):
                    r0 = (ig + ii) * N_KEYS + half * HK
                    rs = slice(r0, r0 + HK)
                    w_s[rs, cs] = accs[ii].astype(BF16) * _gelu2(hid_s[rs, cs])
    acc_s[...] += _dot(vt_ref[...], w_s[...])

    @pl.when(j == pl.num_programs(1) - 1)
    def _fin():
        o_ref[...] = h_ref[...] + acc_s[...].T


def _peer(h, g, wq_t, keys, u, vt, Tt, Ec):
    N, D = h.shape
    nT = Tt // 128
    nI = Ec // N_KEYS
    kern = functools.partial(_peer_kernel, Tt=Tt, Ec=Ec)
    score = pltpu.VMEM((PEER_HEADS, nT, N_KEYS, 128), F32)
    rows = pltpu.VMEM((PEER_HEADS, nT, nI, 128), F32)
    return pl.pallas_call(
        kern,
        grid=(N // Tt, N_EXPERTS // Ec),
        in_specs=[
            pl.BlockSpec((Tt, D), lambda i, j: (i, 0)),
            pl.BlockSpec((1, D), lambda i, j: (0, 0)),
            pl.BlockSpec((PEER_HEADS * 2 * PEER_HALF, D), lambda i, j: (0, 0)),
            pl.BlockSpec((2 * PEER_HEADS, N_KEYS, PEER_HALF), lambda i, j: (0, 0, 0)),
            pl.BlockSpec((Ec, D), lambda i, j: (j, 0)),
            pl.BlockSpec((None, D, Ec), lambda i, j: (j, 0, 0)),
        ],
        out_specs=pl.BlockSpec((Tt, D), lambda i, j: (i, 0)),
        out_shape=jax.ShapeDtypeStruct((N, D), F32),
        scratch_shapes=[
            pltpu.VMEM((D, Tt), BF16),
            score, score, score, score,
            rows, rows,
            pltpu.VMEM((Ec, Tt), BF16),
            pltpu.VMEM((Ec, Tt), BF16),
            pltpu.VMEM((D, Tt), F32),
        ],
        compiler_params=_cparams(("parallel", "arbitrary")),
        name="peer",
    )(h, g, wq_t, keys, u, vt)


def _ple_kernel(h_ref, p_ref, g_ref, pw_ref, gw_ref, fg_ref, o_ref, *, final):
    h = h_ref[...]
    gate = _sigmoid(_dot(_rms(h, g_ref[...]).astype(BF16), gw_ref[...]))
    out = h + _dot(p_ref[...].astype(BF16), pw_ref[...]) * gate
    if final:
        out = _rms(out, fg_ref[...])
    o_ref[...] = out


def _ple(h, p, g, pw, gw, fg, tm, final):
    N, D = h.shape
    return pl.pallas_call(
        functools.partial(_ple_kernel, final=final),
        grid=(N // tm,),
        in_specs=[
            pl.BlockSpec((tm, D), lambda i: (i, 0)),
            pl.BlockSpec((tm, D_PLE), lambda i: (i, 0)),
            pl.BlockSpec((1, D), lambda i: (0, 0)),
            pl.BlockSpec((D_PLE, D), lambda i: (0, 0)),
            pl.BlockSpec((D, D), lambda i: (0, 0)),
            pl.BlockSpec((1, D), lambda i: (0, 0)),
        ],
        out_specs=pl.BlockSpec((tm, D), lambda i: (i, 0)),
        out_shape=jax.ShapeDtypeStruct((N, D), F32),
        compiler_params=_cparams(("parallel",)),
        name="ple",
    )(h, p, g, pw, gw, fg)


def _layer_params(i, norm1_g, w_in, conv_w, conv_b, lru_wa, lru_ba, lru_wx, lru_bx, lru_lambda,
                  rwkv_mu, rwkv_w0, rwkv_w2, rwkv_a0, rwkv_a2, rwkv_g2, rwkv_kk, rwkv_ka, rwkv_rk,
                  rwkv_ln_w, rwkv_ln_b, w_out, norm2_g, peer_wq, peer_keys, peer_u, peer_v,
                  norm3_g, ple_w, ple_gate):
    row = lambda a: a.reshape(1, -1)
    wi = w_in[i].astype(BF16)
    zpad = jnp.zeros((LORA_WA // 2, RWKV_WIDTH), BF16)
    return dict(
        g1=row(norm1_g[i]), wl=wi[:, :LRU_COLS], wr=wi[:, LRU_COLS:],
        lru=dict(cw=conv_w[i], cb=row(conv_b[i]), wa=lru_wa[i].astype(BF16), ba=row(lru_ba[i]),
                 wx=lru_wx[i].astype(BF16), bx=row(lru_bx[i]), lam=row(lru_lambda[i])),
        rwkv=dict(mu=row(rwkv_mu[i]), w0=row(rwkv_w0[i]),
                  w2=jnp.concatenate([rwkv_w2[i].astype(BF16), zpad], axis=0), a0=row(rwkv_a0[i]),
                  a2=jnp.concatenate([zpad, rwkv_a2[i].astype(BF16)], axis=0),
                  g2=rwkv_g2[i].astype(BF16), kk=row(rwkv_kk[i]), ka=row(rwkv_ka[i]),
                  rk=row(rwkv_rk[i]), lnw=row(rwkv_ln_w[i]), lnb=row(rwkv_ln_b[i])),
        wo_a=w_out[i, :LRU_WIDTH].astype(BF16), wo_b=w_out[i, LRU_WIDTH:].astype(BF16),
        g2=row(norm2_g[i]), wq_t=peer_wq[i].T.astype(BF16),
        keys=peer_keys[i].reshape(2 * PEER_HEADS, N_KEYS, PEER_HALF).astype(BF16),
        u=peer_u[i].astype(BF16),
        vt=jnp.swapaxes(peer_v[i].astype(BF16).reshape(N_EXPERTS // PEER_EC, PEER_EC, D_MODEL), 1, 2),
        g3=row(norm3_g[i]), pw=ple_w[i].astype(BF16), gw=ple_gate[i].astype(BF16),
    )


def _pairs_from_state(s):
    B = s.shape[0]
    s = s.reshape(B, RWKV_PAIRS, 2, RWKV_HEAD, RWKV_HEAD)
    z = jnp.zeros_like(s[:, :, 0])
    top = jnp.concatenate([s[:, :, 0], z], axis=-1)
    bot = jnp.concatenate([z, s[:, :, 1]], axis=-1)
    return jnp.concatenate([top, bot], axis=-2)


def _state_from_pairs(w):
    B = w.shape[0]
    w = w.reshape(B, RWKV_PAIRS, 2, RWKV_HEAD, 2, RWKV_HEAD)
    return jnp.stack([w[:, :, 0, :, 0, :], w[:, :, 1, :, 1, :]], axis=2).reshape(
        B, RWKV_HEADS, RWKV_HEAD, RWKV_HEAD)


def _trunk(x, p, conv_st, lru_st, shift_st, wkv_st, first, layers, final_g, tiles):
    B, T, D = x.shape
    depth = len(layers)
    tm, lt = tiles
    tt, ec = PEER_TT, PEER_EC
    h = x
    convs, lrus, shifts, wkvs = [], [], [], []
    for i, lp in enumerate(layers):
        zl, zr = _inproj(h, lp["g1"], lp["wl"], lp["wr"], tm)
        conv0 = jnp.swapaxes(conv_st[i], 0, 1).reshape((CONV_W - 1) * B, LRU_WIDTH)
        y_lru, h_lru = _lru(zl.reshape(T * B, LRU_COLS), conv0, lru_st[i], lp["lru"], B, T, lt, first)
        y_rw, wt = _rwkv(zr, shift_st[i].reshape(B, 1, RWKV_COLS), _pairs_from_state(wkv_st[i]),
                         lp["rwkv"], B, T)
        h1 = _outproj(h, y_lru.reshape(T, B * LRU_WIDTH), y_rw, lp["wo_a"], lp["wo_b"], tm)
        h2 = _peer(h1.reshape(B * T, D), lp["g2"], lp["wq_t"], lp["keys"], lp["u"], lp["vt"], tt, ec)
        h3 = _ple(h2, p[i].reshape(B * T, D_PLE), lp["g3"], lp["pw"], lp["gw"], final_g, tt,
                  final=(i == depth - 1))
        h = h3.reshape(B, T, D)
        zl3 = zl.reshape(T, B, LRU_COLS)
        convs.append(jnp.swapaxes(zl3[T - (CONV_W - 1):, :, :LRU_WIDTH], 0, 1))
        lrus.append(h_lru)
        shifts.append(zr.reshape(T, B, RWKV_COLS)[T - 1])
        wkvs.append(_state_from_pairs(wt))
    return h, jnp.stack(convs), jnp.stack(lrus), jnp.stack(shifts), jnp.stack(wkvs)


def kernel(x_prompt, x_sample, p_prompt, p_sample, state_conv, state_lru, state_shift, state_wkv, norm1_g, w_in, conv_w, conv_b, lru_wa, lru_ba, lru_wx, lru_bx, lru_lambda, rwkv_mu, rwkv_w0, rwkv_w2, rwkv_a0, rwkv_a2, rwkv_g2, rwkv_kk, rwkv_ka, rwkv_rk, rwkv_ln_w, rwkv_ln_b, w_out, norm2_g, peer_wq, peer_keys, peer_u, peer_v, norm3_g, ple_w, ple_gate, final_g):
    depth = w_in.shape[0]
    layers = [_layer_params(i, norm1_g, w_in, conv_w, conv_b, lru_wa, lru_ba, lru_wx, lru_bx,
                            lru_lambda, rwkv_mu, rwkv_w0, rwkv_w2, rwkv_a0, rwkv_a2, rwkv_g2,
                            rwkv_kk, rwkv_ka, rwkv_rk, rwkv_ln_w, rwkv_ln_b, w_out, norm2_g,
                            peer_wq, peer_keys, peer_u, peer_v, norm3_g, ple_w, ple_gate)
              for i in range(depth)]
    fg = final_g.reshape(1, -1)
    bp = x_prompt.shape[0]
    dt = x_prompt.dtype
    zc = jnp.zeros((depth, bp, CONV_W - 1, LRU_WIDTH), dt)
    zl = jnp.zeros((depth, bp, LRU_WIDTH), dt)
    zs = jnp.zeros((depth, bp, RWKV_COLS), dt)
    zw = jnp.zeros((depth, bp, RWKV_HEADS, RWKV_HEAD, RWKV_HEAD), dt)
    tp = min(512, x_prompt.shape[1])
    ts = min(512, x_sample.shape[1])
    out_p = _trunk(x_prompt, p_prompt, zc, zl, zs, zw, True, layers, fg,
                   (tp, min(32, x_prompt.shape[1])))
    out_s = _trunk(x_sample, p_sample, state_conv, state_lru, state_shift, state_wkv, False,
                   layers, fg, (ts, min(32, x_sample.shape[1])))
    return (out_p[0], out_s[0]) + tuple(out_p[1:]) + tuple(out_s[1:])
```

```python
import functools
import math

import jax
import jax.numpy as jnp
from jax import lax
from jax.experimental import pallas as pl
from jax.experimental.pallas import tpu as pltpu

F32 = jnp.float32
BF16 = jnp.bfloat16

D_MODEL = 1024
D_PLE = 256
LRU_WIDTH = 512
LRU_BLOCKS = 4
LRU_BLOCK = 128
CONV_W = 4
LRU_C = 8.0
RWKV_WIDTH = 512
RWKV_HEAD = 64
RWKV_HEADS = 8
RWKV_PAIRS = RWKV_HEADS // 2
LORA_WA = 128
G_LORA = 128
RWKV_COLS = 3 * RWKV_WIDTH + LORA_WA + G_LORA
LRU_COLS = 2 * LRU_WIDTH
PEER_HEADS = 8
N_KEYS = 128
N_EXPERTS = N_KEYS * N_KEYS
PEER_HALF = 128
PEER_TOPK = 16
PEER_TT = 512
PEER_EC = 1024
EPS = 1e-6
GN_EPS = 64e-5

RWKV_CHUNK = 128
VMEM_LIMIT = 56 * 1024 * 1024


def _cparams(sem):
    return pltpu.CompilerParams(dimension_semantics=sem, vmem_limit_bytes=VMEM_LIMIT)


def _rms(x, g):
    return x * lax.rsqrt(jnp.mean(x * x, axis=-1, keepdims=True) + EPS) * g


def _gelu(x):
    c = math.sqrt(2.0 / math.pi)
    return x * (0.5 * (1.0 + jnp.tanh(c * (x + 0.044715 * (x * x * x)))))


def _sigmoid(x):
    return 1.0 / (1.0 + jnp.exp(-x))


def _softplus(x):
    return jnp.maximum(x, 0.0) + jnp.log1p(jnp.exp(-jnp.abs(x)))


def _dot(a, b):
    return jnp.dot(a, b, preferred_element_type=F32)


def _dot_nt(a, b):
    return lax.dot_general(a, b, (((1,), (1,)), ((), ())), preferred_element_type=F32)


def _dot_tn(a, b):
    return lax.dot_general(a, b, (((0,), (0,)), ((), ())), preferred_element_type=F32)


def _inproj_kernel(x_ref, g_ref, wl_ref, wr_ref, zl_ref, zr_ref):
    xn = _rms(x_ref[...], g_ref[...]).astype(BF16)
    zl_ref[...] = _dot(xn, wl_ref[...])
    zr_ref[...] = _dot(xn, wr_ref[...])


def _inproj(x, g, wl, wr, tm):
    B, T, D = x.shape
    return pl.pallas_call(
        _inproj_kernel,
        grid=(B, T // tm),
        in_specs=[
            pl.BlockSpec((None, tm, D), lambda b, i: (b, i, 0)),
            pl.BlockSpec((1, D), lambda b, i: (0, 0)),
            pl.BlockSpec((D, LRU_COLS), lambda b, i: (0, 0)),
            pl.BlockSpec((D, RWKV_COLS), lambda b, i: (0, 0)),
        ],
        out_specs=[
            pl.BlockSpec((tm, LRU_COLS), lambda b, i: (i, b)),
            pl.BlockSpec((tm, RWKV_COLS), lambda b, i: (i, b)),
        ],
        out_shape=[
            jax.ShapeDtypeStruct((T, B * LRU_COLS), F32),
            jax.ShapeDtypeStruct((T, B * RWKV_COLS), F32),
        ],
        compiler_params=_cparams(("parallel", "parallel")),
        name="inproj",
    )(x, g, wl, wr)


def _lru_kernel(z_ref, tail0_ref, h0_ref, cw_ref, cb_ref, wa_ref, ba_ref, wx_ref, bx_ref,
                lam_ref, y_ref, hlast_ref, tail_s, h_s, *, B, Lt, first):
    i = pl.program_id(0)
    W = LRU_WIDTH

    @pl.when(i == 0)
    def _():
        tail_s[...] = tail0_ref[...]
        h_s[...] = h0_ref[...]

    def stack(f):
        return jnp.concatenate([f(b) for b in range(B)], axis=0)

    def per_stream(rows):
        return stack(lambda b: jnp.broadcast_to(rows[b:b + 1, :], (Lt, W)))

    xb = stack(lambda b: z_ref[:, b * LRU_COLS:b * LRU_COLS + W])
    gb = stack(lambda b: z_ref[:, b * LRU_COLS + W:(b + 1) * LRU_COLS])
    tt = lax.broadcasted_iota(jnp.int32, (Lt, 1), 0)
    tt = jnp.concatenate([tt] * B, axis=0)
    xc = cb_ref[...] + xb * cw_ref[CONV_W - 1:CONV_W, :]
    for d in range(1, CONV_W):
        xs = pltpu.roll(xb, d, 0)
        for t0 in range(d):
            r = 8 - d + t0
            prev = stack(lambda b: jnp.broadcast_to(tail_s[r:r + 1, b * LRU_COLS:b * LRU_COLS + W],
                                                    (Lt, W)))
            xs = jnp.where(tt == t0, prev, xs)
        xc = xc + xs * cw_ref[CONV_W - 1 - d:CONV_W - d, :]
    tail_s[...] = z_ref[Lt - 8:Lt, :]

    ra, ix = [], []
    for n in range(LRU_BLOCKS):
        xn = xc[:, n * LRU_BLOCK:(n + 1) * LRU_BLOCK].astype(BF16)
        ra.append(_dot(xn, wa_ref[n]))
        ix.append(_dot(xn, wx_ref[n]))
    r = _sigmoid(jnp.concatenate(ra, axis=1) + ba_ref[...])
    ig = _sigmoid(jnp.concatenate(ix, axis=1) + bx_ref[...])
    log_a = -LRU_C * r * _softplus(-lam_ref[...])
    a = jnp.exp(log_a)
    t = jnp.tanh(log_a)
    mult = jnp.sqrt(-2.0 * t / (1.0 - t))
    if first:
        mult = jnp.where(tt < jnp.where(i == 0, 1, 0), 1.0, mult)
    bv = mult * (ig * xc)
    s = 1
    while s < Lt:
        m = tt >= s
        a_sh = pltpu.roll(a, s, 0)
        b_sh = pltpu.roll(bv, s, 0)
        bv = jnp.where(m, a * b_sh + bv, bv)
        a = jnp.where(m, a * a_sh, a)
        s *= 2
    h = a * per_stream(h_s[...]) + bv
    hl = jnp.concatenate([h[b * Lt + Lt - 1:b * Lt + Lt, :] for b in range(B)], axis=0)
    h_s[...] = hl
    hlast_ref[...] = hl
    y = h * _gelu(gb)
    for b in range(B):
        y_ref[:, b * W:(b + 1) * W] = y[b * Lt:(b + 1) * Lt, :]


def _lru(zl, tail0, h0, p, B, T, Lt, first):
    W = LRU_WIDTH
    kern = functools.partial(_lru_kernel, B=B, Lt=Lt, first=first)
    full = lambda shape: pl.BlockSpec(shape, lambda i: (0,) * len(shape))
    return pl.pallas_call(
        kern,
        grid=(T // Lt,),
        in_specs=[
            pl.BlockSpec((Lt, B * LRU_COLS), lambda i: (i, 0)),
            full((8, B * LRU_COLS)),
            full((B, W)),
            full((CONV_W, W)),
            full((1, W)),
            full((LRU_BLOCKS, LRU_BLOCK, LRU_BLOCK)),
            full((1, W)),
            full((LRU_BLOCKS, LRU_BLOCK, LRU_BLOCK)),
            full((1, W)),
            full((1, W)),
        ],
        out_specs=[
            pl.BlockSpec((Lt, B * W), lambda i: (i, 0)),
            full((B, W)),
        ],
        out_shape=[
            jax.ShapeDtypeStruct((T, B * W), F32),
            jax.ShapeDtypeStruct((B, W), F32),
        ],
        scratch_shapes=[
            pltpu.VMEM((8, B * LRU_COLS), F32),
            pltpu.VMEM((B, W), F32),
        ],
        compiler_params=_cparams(("arbitrary",)),
        name="rglru",
    )(zl, tail0, h0, p["cw"], p["cb"], p["wa"], p["ba"], p["wx"], p["bx"], p["lam"])


def _rwkv_kernel(z_ref, sh0_ref, wt0_ref, mu_ref, w0_ref, w2_ref, a0_ref, a2_ref, g2_ref,
                 kk_ref, ka_ref, rk_ref, lnw_ref, lnb_ref, y_ref, wt_ref, zprev_s, wt_s, *, Tv):
    L = RWKV_CHUNK
    Wd = RWKV_WIDTH
    c = pl.program_id(1)

    @pl.when(c == 0)
    def _():
        zprev_s[...] = sh0_ref[...]
        wt_s[...] = wt0_ref[...]

    z = z_ref[...]
    if Tv < L:
        z = jnp.concatenate([z, jnp.zeros((L - Tv, RWKV_COLS), F32)], axis=0)
    row = lax.broadcasted_iota(jnp.int32, (L, 1), 0)
    zp = jnp.where(row == 0, zprev_s[...], pltpu.roll(z, 1, 0))
    zprev_s[...] = z[Tv - 1:Tv, :]
    zs = z + (zp - z) * mu_ref[...]
    r = zs[:, 0:Wd]
    k = zs[:, Wd:2 * Wd]
    v = zs[:, 2 * Wd:3 * Wd]
    zwa = zs[:, 3 * Wd:3 * Wd + LORA_WA]
    zg = zs[:, 3 * Wd + LORA_WA:]
    if Tv < L:
        valid = row < Tv
        r = jnp.where(valid, r, 0.0)
        k = jnp.where(valid, k, 0.0)
        v = jnp.where(valid, v, 0.0)
    w_log = -_softplus(-(w0_ref[...] + _dot(jnp.tanh(zwa).astype(BF16), w2_ref[...]))) - 0.5
    ld = -jnp.exp(w_log)
    if Tv < L:
        ld = jnp.where(valid, ld, 0.0)
    iclr = _sigmoid(a0_ref[...] + _dot(zwa.astype(BF16), a2_ref[...]))
    g = _dot(_sigmoid(zg).astype(BF16), g2_ref[...])
    kkraw = k * kk_ref[...]
    kmod = k * (1.0 + (iclr - 1.0) * ka_ref[...])

    ri = lax.broadcasted_iota(jnp.int32, (L, L), 0)
    ci = lax.broadcasted_iota(jnp.int32, (L, L), 1)
    strict = ri > ci
    incl = ri >= ci
    tri = jnp.where(incl, 1.0, 0.0).astype(BF16)
    ld_hi = ld.astype(BF16)
    rem = ld - ld_hi.astype(F32)
    ld_mid = rem.astype(BF16)
    ld_lo = (rem - ld_mid.astype(F32)).astype(BF16)
    cum = _dot(tri, ld_hi) + _dot(tri, ld_mid) + _dot(tri, ld_lo)
    cum_l = cum[L - 1:L, :]
    cumm = cum - cum[L // 2 - 1:L // 2, :]
    e_m = jnp.exp(cumm)
    e_mi = jnp.exp(-cumm)
    e_mprev = jnp.exp(cumm - ld)
    e_0 = jnp.exp(cum)
    e_0prev = jnp.exp(cum - ld)
    e_end = jnp.exp(cum_l - cum)
    p_l = jnp.exp(cum_l)
    lane = lax.broadcasted_iota(jnp.int32, (1, 2 * RWKV_HEAD), 1)
    m0 = lane < RWKV_HEAD
    bi = lax.broadcasted_iota(jnp.int32, (2 * RWKV_HEAD, 2 * RWKV_HEAD), 0) < RWKV_HEAD
    bj = lax.broadcasted_iota(jnp.int32, (2 * RWKV_HEAD, 2 * RWKV_HEAD), 1) < RWKV_HEAD
    blockdiag = bi == bj

    def headsum(x):
        s0 = jnp.sum(jnp.where(m0, x, 0.0), axis=-1, keepdims=True)
        s1 = jnp.sum(jnp.where(m0, 0.0, x), axis=-1, keepdims=True)
        return jnp.where(m0, s0, s1)

    P = range(RWKV_PAIRS)
    sls = [slice(p * 128, (p + 1) * 128) for p in P]
    pre = []
    for p in P:
        sl = sls[p]
        r_p, v_p, kmod_p, iclr_p = r[:, sl], v[:, sl], kmod[:, sl], iclr[:, sl]
        kkr = kkraw[:, sl]
        kk = kkr / jnp.maximum(jnp.sqrt(headsum(kkr * kkr)), 1e-12)
        a_p = -kk
        b_p = kk * iclr_p
        at = a_p * e_mprev[:, sl]
        rt = r_p * e_m[:, sl]
        bk = jnp.concatenate([b_p * e_mi[:, sl], kmod_p * e_mi[:, sl]], axis=0).astype(BF16)
        ar0 = jnp.concatenate([a_p * e_0prev[:, sl], r_p * e_0[:, sl]], axis=0).astype(BF16)
        be = jnp.concatenate([b_p * e_end[:, sl], kmod_p * e_end[:, sl]], axis=0).astype(BF16)
        pre.append(dict(r=r_p, v=v_p, kmod=kmod_p, at=at, rt=rt, bk=bk, ar0=ar0, be=be,
                        vb=v_p.astype(BF16)))
    wts = [wt_s[p] for p in P]
    as0 = [_dot_nt(pre[p]["ar0"], wts[p].astype(BF16)) for p in P]
    heads = [(p, h) for p in P for h in range(2)]
    gms = {}
    for (p, h) in heads:
        mh = m0 if h == 0 else jnp.logical_not(m0)
        arm = jnp.concatenate([jnp.where(mh, pre[p]["at"], 0.0), jnp.where(mh, pre[p]["rt"], 0.0)],
                              axis=0)
        gms[(p, h)] = _dot_nt(arm.astype(BF16), pre[p]["bk"])
    nab, mm, ys = {}, {}, {}
    for (p, h) in heads:
        gm = gms[(p, h)]
        nab[(p, h)] = jnp.where(strict, gm[0:L, 0:L], 0.0).astype(BF16)
        n_ak = jnp.where(strict, gm[0:L, L:2 * L], 0.0)
        mm[(p, h)] = jnp.concatenate([jnp.where(incl, gm[L:2 * L, 0:L], 0.0),
                                      jnp.where(incl, gm[L:2 * L, L:2 * L], 0.0)],
                                     axis=1).astype(BF16)
        ys[(p, h)] = as0[p][0:L] + _dot(n_ak.astype(BF16), pre[p]["vb"])
    pws = dict(nab)
    nlev = int(math.log2(L))
    for lev in range(nlev):
        for hd in heads:
            ys[hd] = ys[hd] + _dot(pws[hd], ys[hd].astype(BF16))
        if lev + 1 < nlev:
            for hd in heads:
                pws[hd] = _dot(pws[hd], pws[hd]).astype(BF16)
    for p in P:
        sl = sls[p]
        pp = pre[p]
        u_p = jnp.where(m0, ys[(p, 0)], ys[(p, 1)])
        uv = jnp.concatenate([u_p, pp["v"]], axis=0).astype(BF16)
        o0 = as0[p][L:2 * L] + _dot(mm[(p, 0)], uv)
        o1 = as0[p][L:2 * L] + _dot(mm[(p, 1)], uv)
        o = jnp.where(m0, o0, o1)
        wt_new = wts[p] * p_l[:, sl] + jnp.where(blockdiag, _dot_tn(uv, pp["be"]), 0.0)
        wt_s[p] = wt_new
        wt_ref[p] = wt_new
        mean = headsum(o) * (1.0 / RWKV_HEAD)
        d = o - mean
        var = headsum(d * d) * (1.0 / RWKV_HEAD)
        on = d * lax.rsqrt(var + GN_EPS) * lnw_ref[:, sl] + lnb_ref[:, sl]
        bonus = headsum(pp["r"] * pp["kmod"] * rk_ref[:, sl]) * pp["v"]
        y = (on + bonus) * g[:, sl]
        y_ref[:, sl] = y[0:Tv]


def _rwkv(zr, sh0, wt0, p, B, T):
    Tv = min(T, RWKV_CHUNK)
    kern = functools.partial(_rwkv_kernel, Tv=Tv)
    row = lambda n: pl.BlockSpec((1, n), lambda b, c: (0, 0))
    mat = lambda m, n: pl.BlockSpec((m, n), lambda b, c: (0, 0))
    st = pl.BlockSpec((None, RWKV_PAIRS, 128, 128), lambda b, c: (b, 0, 0, 0))
    return pl.pallas_call(
        kern,
        grid=(B, T // Tv),
        in_specs=[
            pl.BlockSpec((Tv, RWKV_COLS), lambda b, c: (c, b)),
            pl.BlockSpec((None, 1, RWKV_COLS), lambda b, c: (b, 0, 0)),
            st,
            row(RWKV_COLS), row(RWKV_WIDTH), mat(LORA_WA, RWKV_WIDTH), row(RWKV_WIDTH),
            mat(LORA_WA, RWKV_WIDTH), mat(G_LORA, RWKV_WIDTH),
            row(RWKV_WIDTH), row(RWKV_WIDTH), row(RWKV_WIDTH), row(RWKV_WIDTH), row(RWKV_WIDTH),
        ],
        out_specs=[
            pl.BlockSpec((Tv, RWKV_WIDTH), lambda b, c: (c, b)),
            st,
        ],
        out_shape=[
            jax.ShapeDtypeStruct((T, B * RWKV_WIDTH), F32),
            jax.ShapeDtypeStruct((B, RWKV_PAIRS, 128, 128), F32),
        ],
        scratch_shapes=[
            pltpu.VMEM((1, RWKV_COLS), F32),
            pltpu.VMEM((RWKV_PAIRS, 128, 128), F32),
        ],
        compiler_params=_cparams(("arbitrary", "arbitrary")),
        name="rwkv7",
    )(zr, sh0, wt0, p["mu"], p["w0"], p["w2"], p["a0"], p["a2"], p["g2"],
      p["kk"], p["ka"], p["rk"], p["lnw"], p["lnb"])


def _outproj_kernel(x_ref, yl_ref, yr_ref, wa_ref, wb_ref, o_ref):
    o_ref[...] = (x_ref[...] + _dot(yl_ref[...].astype(BF16), wa_ref[...])
                  + _dot(yr_ref[...].astype(BF16), wb_ref[...]))


def _outproj(x, yl, yr, wa, wb, tm):
    B, T, D = x.shape
    return pl.pallas_call(
        _outproj_kernel,
        grid=(B, T // tm),
        in_specs=[
            pl.BlockSpec((None, tm, D), lambda b, i: (b, i, 0)),
            pl.BlockSpec((tm, LRU_WIDTH), lambda b, i: (i, b)),
            pl.BlockSpec((tm, RWKV_WIDTH), lambda b, i: (i, b)),
            pl.BlockSpec((LRU_WIDTH, D), lambda b, i: (0, 0)),
            pl.BlockSpec((RWKV_WIDTH, D), lambda b, i: (0, 0)),
        ],
        out_specs=pl.BlockSpec((None, tm, D), lambda b, i: (b, i, 0)),
        out_shape=jax.ShapeDtypeStruct((B, T, D), F32),
        compiler_params=_cparams(("parallel", "parallel")),
        name="outproj",
    )(x, yl, yr, wa, wb)


def _sort16_pairs():
    pairs = []

    def merge(lo, n, r):
        step = 2 * r
        if step < n:
            merge(lo, n, step)
            merge(lo + r, n, step)
            for i in range(lo + r, lo + n - r, step):
                pairs.append((i, i + r))
        else:
            pairs.append((lo, lo + r))

    def sort(lo, n):
        if n > 1:
            sort(lo, n // 2)
            sort(lo + n // 2, n // 2)
            merge(lo, n, 1)

    sort(0, 16)
    return pairs


_SORT16 = _sort16_pairs()


def _merge_top(g, n):
    g = list(g)
    S = g[0].shape[0]
    sub = lax.broadcasted_iota(jnp.int32, g[0].shape, 0).astype(F32)
    rows = []
    for it in range(n):
        m = jnp.max(g[0], axis=0, keepdims=True)
        rows.append(m)
        if it + 1 < n:
            first = jnp.min(jnp.where(g[0] == m, sub, float(S)), axis=0, keepdims=True)
            sel = sub == first
            for v in range(min(n - 1 - it, len(g) - 1)):
                g[v] = jnp.where(sel, g[v + 1], g[v])
            if len(g) <= n - 1 - it:
                g[-1] = jnp.where(sel, -jnp.inf, g[-1])
    return rows


def _top16_rows(x):
    g = [x[8 * v:8 * v + 8, :] for v in range(16)]
    for (i, j) in _SORT16:
        g[i], g[j] = jnp.maximum(g[i], g[j]), jnp.minimum(g[i], g[j])
    return _merge_top(g, PEER_TOPK)


def _gelu2(x):
    c = math.sqrt(2.0 / math.pi)
    inner = x * (c + (c * 0.044715) * (x * x))
    hx = 0.5 * x
    return hx + hx * jnp.tanh(inner)


def _peer_kernel(h_ref, g_ref, wq_ref, keys_ref, u_ref, vt_ref, o_ref,
                 xnt_s, s1_s, s2_s, tau_s, s1x_s, w_s, hid_s, acc_s, *, Tt, Ec):
    j = pl.program_id(1)
    nI = Ec // N_KEYS
    nT = Tt // 128
    Kk = PEER_TOPK
    log2e = 1.0 / math.log(2.0)

    @pl.when(j == 0)
    def _prep():
        xn = _rms(h_ref[...], g_ref[...])
        xnb = xn.astype(BF16)
        xnt_s[...] = xn.T.astype(BF16)
        acc_s[...] = jnp.zeros_like(acc_s)
        sub = lax.broadcasted_iota(jnp.int32, (Kk, Tt), 0)

        def cands(a1, t2):
            return [jnp.where(sub < Kk // (q + 1), a1 + t2[q], -jnp.inf) for q in range(Kk)]

        for h in range(PEER_HEADS):
            q = _dot_nt(wq_ref[h * 2 * PEER_HALF:(h + 1) * 2 * PEER_HALF, :], xnb)
            s1 = _dot(keys_ref[2 * h], q[0:PEER_HALF].astype(BF16))
            s2 = _dot(keys_ref[2 * h + 1], q[PEER_HALF:].astype(BF16))
            t1 = _top16_rows(s1)
            t2 = _top16_rows(s2)
            sc1 = (s1 - t1[0]) * log2e
            sc2 = (s2 - t2[0]) * log2e
            a1 = (jnp.concatenate(t1, axis=0) - t1[0]) * log2e
            a2 = [(t - t2[0]) * log2e for t in t2]
            zsum = jnp.zeros_like(t1[0])
            for b_ in _merge_top(cands(a1, a2), Kk):
                zsum = zsum + jnp.exp2(b_)
            lz = jnp.log2(zsum)
            sc2 = sc2 - lz
            a2 = [t - lz for t in a2]
            tau_s[h:h + 1, :] = _merge_top(cands(a1, a2), Kk)[Kk - 1]
            for tb in range(nT):
                cs = slice(tb * 128, (tb + 1) * 128)
                s1_s[h, tb] = sc1[:, cs]
                s2_s[h, tb] = sc2[:, cs]

    HK = N_KEYS // 2
    IG = 2
    r0d = pl.multiple_of(j * nI, nI)
    for h in range(PEER_HEADS):
        for tb in range(nT):
            s1x_s[h, tb] = s1_s[h, tb, pl.ds(r0d, nI), :]
    hid_s[...] = _dot(u_ref[...], xnt_s[...])
    for tb in range(nT):
        cs = slice(tb * 128, (tb + 1) * 128)
        for half in range(2):
            ks = slice(half * HK, (half + 1) * HK)
            for ig in range(0, nI, IG):
                accs = [jnp.zeros((HK, 128), F32) for _ in range(IG)]
                for h in range(PEER_HEADS):
                    s2h = s2_s[h, tb, ks, :]
                    tau = tau_s[h:h + 1, cs]
                    for ii in range(IG):
                        i1 = ig + ii
                        cval = s1x_s[h, tb, i1:i1 + 1, :] + s2h
                        accs[ii] = accs[ii] + jnp.where(cval >= tau, jnp.exp2(cval), 0.0)
                for ii in range(IG):
                    r0 = (ig + ii) * N_KEYS + half * HK
                    rs = slice(r0, r0 + HK)
                    w_s[rs, cs] = (accs[ii] * _gelu2(hid_s[rs, cs])).astype(BF16)
    acc_s[...] += _dot(vt_ref[...], w_s[...])

    @pl.when(j == pl.num_programs(1) - 1)
    def _fin():
        o_ref[...] = h_ref[...] + acc_s[...].T


def _peer(h, g, wq_t, keys, u, vt, Tt, Ec):
    N, D = h.shape
    nT = Tt // 128
    nI = Ec // N_KEYS
    kern = functools.partial(_peer_kernel, Tt=Tt, Ec=Ec)
    score = pltpu.VMEM((PEER_HEADS, nT, N_KEYS, 128), F32)
    rows = pltpu.VMEM((PEER_HEADS, nT, nI, 128), F32)
    return pl.pallas_call(
        kern,
        grid=(N // Tt, N_EXPERTS // Ec),
        in_specs=[
            pl.BlockSpec((Tt, D), lambda i, j: (i, 0)),
            pl.BlockSpec((1, D), lambda i, j: (0, 0)),
            pl.BlockSpec((PEER_HEADS * 2 * PEER_HALF, D), lambda i, j: (0, 0)),
            pl.BlockSpec((2 * PEER_HEADS, N_KEYS, PEER_HALF), lambda i, j: (0, 0, 0)),
            pl.BlockSpec((Ec, D), lambda i, j: (j, 0)),
            pl.BlockSpec((D, Ec), lambda i, j: (0, j)),
        ],
        out_specs=pl.BlockSpec((Tt, D), lambda i, j: (i, 0)),
        out_shape=jax.ShapeDtypeStruct((N, D), F32),
        scratch_shapes=[
            pltpu.VMEM((D, Tt), BF16),
            score, score,
            pltpu.VMEM((PEER_HEADS, Tt), F32),
            rows,
            pltpu.VMEM((Ec, Tt), BF16),
            pltpu.VMEM((Ec, Tt), F32),
            pltpu.VMEM((D, Tt), F32),
        ],
        compiler_params=_cparams(("parallel", "arbitrary")),
        name="peer",
    )(h, g, wq_t, keys, u, vt)


def _ple_kernel(h_ref, p_ref, g_ref, pw_ref, gw_ref, fg_ref, o_ref, *, final):
    h = h_ref[...]
    gate = _sigmoid(_dot(_rms(h, g_ref[...]).astype(BF16), gw_ref[...]))
    out = h + _dot(p_ref[...].astype(BF16), pw_ref[...]) * gate
    if final:
        out = _rms(out, fg_ref[...])
    o_ref[...] = out


def _ple(h, p, g, pw, gw, fg, tm, final):
    N, D = h.shape
    return pl.pallas_call(
        functools.partial(_ple_kernel, final=final),
        grid=(N // tm,),
        in_specs=[
            pl.BlockSpec((tm, D), lambda i: (i, 0)),
            pl.BlockSpec((tm, D_PLE), lambda i: (i, 0)),
            pl.BlockSpec((1, D), lambda i: (0, 0)),
            pl.BlockSpec((D_PLE, D), lambda i: (0, 0)),
            pl.BlockSpec((D, D), lambda i: (0, 0)),
            pl.BlockSpec((1, D), lambda i: (0, 0)),
        ],
        out_specs=pl.BlockSpec((tm, D), lambda i: (i, 0)),
        out_shape=jax.ShapeDtypeStruct((N, D), F32),
        compiler_params=_cparams(("parallel",)),
        name="ple",
    )(h, p, g, pw, gw, fg)


def _layer_params(i, norm1_g, w_in, conv_w, conv_b, lru_wa, lru_ba, lru_wx, lru_bx, lru_lambda,
                  rwkv_mu, rwkv_w0, rwkv_w2, rwkv_a0, rwkv_a2, rwkv_g2, rwkv_kk, rwkv_ka, rwkv_rk,
                  rwkv_ln_w, rwkv_ln_b, w_out, norm2_g, peer_wq, peer_keys, peer_u, peer_v,
                  norm3_g, ple_w, ple_gate):
    row = lambda a: a.reshape(1, -1)
    wi = w_in[i].astype(BF16)
    zpad = jnp.zeros((LORA_WA // 2, RWKV_WIDTH), BF16)
    return dict(
        g1=row(norm1_g[i]), wl=wi[:, :LRU_COLS], wr=wi[:, LRU_COLS:],
        lru=dict(cw=conv_w[i], cb=row(conv_b[i]), wa=lru_wa[i].astype(BF16), ba=row(lru_ba[i]),
                 wx=lru_wx[i].astype(BF16), bx=row(lru_bx[i]), lam=row(lru_lambda[i])),
        rwkv=dict(mu=row(rwkv_mu[i]), w0=row(rwkv_w0[i]),
                  w2=jnp.concatenate([rwkv_w2[i].astype(BF16), zpad], axis=0), a0=row(rwkv_a0[i]),
                  a2=jnp.concatenate([zpad, rwkv_a2[i].astype(BF16)], axis=0),
                  g2=rwkv_g2[i].astype(BF16), kk=row(rwkv_kk[i]), ka=row(rwkv_ka[i]),
                  rk=row(rwkv_rk[i]), lnw=row(rwkv_ln_w[i]), lnb=row(rwkv_ln_b[i])),
        wo_a=w_out[i, :LRU_WIDTH].astype(BF16), wo_b=w_out[i, LRU_WIDTH:].astype(BF16),
        g2=row(norm2_g[i]), wq_t=peer_wq[i].T.astype(BF16),
        keys=peer_keys[i].reshape(2 * PEER_HEADS, N_KEYS, PEER_HALF).astype(BF16),
        u=peer_u[i].astype(BF16), vt=peer_v[i].T.astype(BF16),
        g3=row(norm3_g[i]), pw=ple_w[i].astype(BF16), gw=ple_gate[i].astype(BF16),
    )


def _pairs_from_state(s):
    B = s.shape[0]
    s = s.reshape(B, RWKV_PAIRS, 2, RWKV_HEAD, RWKV_HEAD)
    z = jnp.zeros_like(s[:, :, 0])
    top = jnp.concatenate([s[:, :, 0], z], axis=-1)
    bot = jnp.concatenate([z, s[:, :, 1]], axis=-1)
    return jnp.concatenate([top, bot], axis=-2)


def _state_from_pairs(w):
    B = w.shape[0]
    w = w.reshape(B, RWKV_PAIRS, 2, RWKV_HEAD, 2, RWKV_HEAD)
    return jnp.stack([w[:, :, 0, :, 0, :], w[:, :, 1, :, 1, :]], axis=2).reshape(
        B, RWKV_HEADS, RWKV_HEAD, RWKV_HEAD)


def _trunk(x, p, conv_st, lru_st, shift_st, wkv_st, first, layers, final_g, tiles):
    B, T, D = x.shape
    depth = len(layers)
    tm, lt = tiles
    tt, ec = PEER_TT, PEER_EC
    h = x
    convs, lrus, shifts, wkvs = [], [], [], []
    for i, lp in enumerate(layers):
        zl, zr = _inproj(h, lp["g1"], lp["wl"], lp["wr"], tm)
        tail0 = jnp.pad(jnp.swapaxes(conv_st[i], 0, 1),
                        ((8 - (CONV_W - 1), 0), (0, 0), (0, LRU_COLS - LRU_WIDTH))).reshape(8, B * LRU_COLS)
        y_lru, h_lru = _lru(zl, tail0, lru_st[i], lp["lru"], B, T, lt, first)
        y_rw, wt = _rwkv(zr, shift_st[i].reshape(B, 1, RWKV_COLS), _pairs_from_state(wkv_st[i]),
                         lp["rwkv"], B, T)
        h1 = _outproj(h, y_lru, y_rw, lp["wo_a"], lp["wo_b"], tm)
        h2 = _peer(h1.reshape(B * T, D), lp["g2"], lp["wq_t"], lp["keys"], lp["u"], lp["vt"], tt, ec)
        h3 = _ple(h2, p[i].reshape(B * T, D_PLE), lp["g3"], lp["pw"], lp["gw"], final_g, tt,
                  final=(i == depth - 1))
        h = h3.reshape(B, T, D)
        zl3 = zl.reshape(T, B, LRU_COLS)
        convs.append(jnp.swapaxes(zl3[T - (CONV_W - 1):, :, :LRU_WIDTH], 0, 1))
        lrus.append(h_lru)
        shifts.append(zr.reshape(T, B, RWKV_COLS)[T - 1])
        wkvs.append(_state_from_pairs(wt))
    return h, jnp.stack(convs), jnp.stack(lrus), jnp.stack(shifts), jnp.stack(wkvs)


def kernel(x_prompt, x_sample, p_prompt, p_sample, state_conv, state_lru, state_shift, state_wkv, norm1_g, w_in, conv_w, conv_b, lru_wa, lru_ba, lru_wx, lru_bx, lru_lambda, rwkv_mu, rwkv_w0, rwkv_w2, rwkv_a0, rwkv_a2, rwkv_g2, rwkv_kk, rwkv_ka, rwkv_rk, rwkv_ln_w, rwkv_ln_b, w_out, norm2_g, peer_wq, peer_keys, peer_u, peer_v, norm3_g, ple_w, ple_gate, final_g):
    depth = w_in.shape[0]
    layers = [_layer_params(i, norm1_g, w_in, conv_w, conv_b, lru_wa, lru_ba, lru_wx, lru_bx,
                            lru_lambda, rwkv_mu, rwkv_w0, rwkv_w2, rwkv_a0, rwkv_a2, rwkv_g2,
                            rwkv_kk, rwkv_ka, rwkv_rk, rwkv_ln_w, rwkv_ln_b, w_out, norm2_g,
                            peer_wq, peer_keys, peer_u, peer_v, norm3_g, ple_w, ple_gate)
              for i in range(depth)]
    fg = final_g.reshape(1, -1)
    bp = x_prompt.shape[0]
    dt = x_prompt.dtype
    zc = jnp.zeros((depth, bp, CONV_W - 1, LRU_WIDTH), dt)
    zl = jnp.zeros((depth, bp, LRU_WIDTH), dt)
    zs = jnp.zeros((depth, bp, RWKV_COLS), dt)
    zw = jnp.zeros((depth, bp, RWKV_HEADS, RWKV_HEAD, RWKV_HEAD), dt)
    tp = min(512, x_prompt.shape[1])
    ts = min(512, x_sample.shape[1])
    out_p = _trunk(x_prompt, p_prompt, zc, zl, zs, zw, True, layers, fg,
                   (tp, min(32, x_prompt.shape[1])))
    out_s = _trunk(x_sample, p_sample, state_conv, state_lru, state_shift, state_wkv, False,
                   layers, fg, (ts, min(32, x_sample.shape[1])))
    return (out_p[0], out_s[0]) + tuple(out_p[1:]) + tuple(out_s[1:])
```

```python
import functools
import math

import jax
import jax.numpy as jnp
from jax import lax
from jax.experimental import pallas as pl
from jax.experimental.pallas import tpu as pltpu

F32 = jnp.float32
BF16 = jnp.bfloat16

D_MODEL = 1024
D_PLE = 256
LRU_WIDTH = 512
LRU_BLOCKS = 4
LRU_BLOCK = 128
CONV_W = 4
LRU_C = 8.0
RWKV_WIDTH = 512
RWKV_HEAD = 64
RWKV_HEADS = 8
RWKV_PAIRS = RWKV_HEADS // 2
LORA_WA = 128
G_LORA = 128
RWKV_COLS = 3 * RWKV_WIDTH + LORA_WA + G_LORA
LRU_COLS = 2 * LRU_WIDTH
PEER_HEADS = 8
N_KEYS = 128
N_EXPERTS = N_KEYS * N_KEYS
PEER_HALF = 128
PEER_TOPK = 16
PEER_TT = 512
PEER_EC = 1024
EPS = 1e-6
GN_EPS = 64e-5

RWKV_CHUNK = 128
RWKV_NB = 2
VMEM_LIMIT = 56 * 1024 * 1024


def _cparams(sem):
    return pltpu.CompilerParams(dimension_semantics=sem, vmem_limit_bytes=VMEM_LIMIT)


def _rms(x, g):
    return x * lax.rsqrt(jnp.mean(x * x, axis=-1, keepdims=True) + EPS) * g


def _gelu(x):
    c = math.sqrt(2.0 / math.pi)
    return x * (0.5 * (1.0 + jnp.tanh(c * (x + 0.044715 * (x * x * x)))))


def _sigmoid(x):
    return 1.0 / (1.0 + jnp.exp(-x))


def _softplus(x):
    return jnp.maximum(x, 0.0) + jnp.log1p(jnp.exp(-jnp.abs(x)))


def _dot(a, b):
    return jnp.dot(a, b, preferred_element_type=F32)


def _dot_nt(a, b):
    return lax.dot_general(a, b, (((1,), (1,)), ((), ())), preferred_element_type=F32)


def _dot_tn(a, b):
    return lax.dot_general(a, b, (((0,), (0,)), ((), ())), preferred_element_type=F32)


def _inproj_kernel(x_ref, g_ref, wl_ref, wr_ref, zl_ref, zr_ref):
    xn = _rms(x_ref[...], g_ref[...]).astype(BF16)
    zl_ref[...] = _dot(xn, wl_ref[...])
    zr_ref[...] = _dot(xn, wr_ref[...])


def _inproj(x, g, wl, wr, tm):
    B, T, D = x.shape
    return pl.pallas_call(
        _inproj_kernel,
        grid=(B, T // tm),
        in_specs=[
            pl.BlockSpec((None, tm, D), lambda b, i: (b, i, 0)),
            pl.BlockSpec((1, D), lambda b, i: (0, 0)),
            pl.BlockSpec((D, LRU_COLS), lambda b, i: (0, 0)),
            pl.BlockSpec((D, RWKV_COLS), lambda b, i: (0, 0)),
        ],
        out_specs=[
            pl.BlockSpec((tm, LRU_COLS), lambda b, i: (i, b)),
            pl.BlockSpec((tm, RWKV_COLS), lambda b, i: (i, b)),
        ],
        out_shape=[
            jax.ShapeDtypeStruct((T, B * LRU_COLS), F32),
            jax.ShapeDtypeStruct((T, B * RWKV_COLS), F32),
        ],
        compiler_params=_cparams(("parallel", "parallel")),
        name="inproj",
    )(x, g, wl, wr)


def _lru_kernel(z_ref, tail0_ref, h0_ref, cw_ref, cb_ref, wa_ref, ba_ref, wx_ref, bx_ref,
                lam_ref, y_ref, hlast_ref, tail_s, h_s, *, B, Lt, first):
    i = pl.program_id(0)
    W = LRU_WIDTH

    @pl.when(i == 0)
    def _():
        tail_s[...] = tail0_ref[...]
        h_s[...] = h0_ref[...]

    def stack(f):
        return jnp.concatenate([f(b) for b in range(B)], axis=0)

    def per_stream(rows):
        return stack(lambda b: jnp.broadcast_to(rows[b:b + 1, :], (Lt, W)))

    xb = stack(lambda b: z_ref[:, b * LRU_COLS:b * LRU_COLS + W])
    gb = stack(lambda b: z_ref[:, b * LRU_COLS + W:(b + 1) * LRU_COLS])
    tt = lax.broadcasted_iota(jnp.int32, (Lt, 1), 0)
    tt = jnp.concatenate([tt] * B, axis=0)
    xc = cb_ref[...] + xb * cw_ref[CONV_W - 1:CONV_W, :]
    for d in range(1, CONV_W):
        xs = pltpu.roll(xb, d, 0)
        for t0 in range(d):
            r = 8 - d + t0
            prev = stack(lambda b: jnp.broadcast_to(tail_s[r:r + 1, b * LRU_COLS:b * LRU_COLS + W],
                                                    (Lt, W)))
            xs = jnp.where(tt == t0, prev, xs)
        xc = xc + xs * cw_ref[CONV_W - 1 - d:CONV_W - d, :]
    tail_s[...] = z_ref[Lt - 8:Lt, :]

    ra, ix = [], []
    for n in range(LRU_BLOCKS):
        xn = xc[:, n * LRU_BLOCK:(n + 1) * LRU_BLOCK].astype(BF16)
        ra.append(_dot(xn, wa_ref[n]))
        ix.append(_dot(xn, wx_ref[n]))
    r = _sigmoid(jnp.concatenate(ra, axis=1) + ba_ref[...])
    ig = _sigmoid(jnp.concatenate(ix, axis=1) + bx_ref[...])
    log_a = -LRU_C * r * _softplus(-lam_ref[...])
    a = jnp.exp(log_a)
    t = jnp.tanh(log_a)
    mult = jnp.sqrt(-2.0 * t / (1.0 - t))
    if first:
        mult = jnp.where(tt < jnp.where(i == 0, 1, 0), 1.0, mult)
    bv = mult * (ig * xc)
    s = 1
    while s < Lt:
        m = tt >= s
        a_sh = pltpu.roll(a, s, 0)
        b_sh = pltpu.roll(bv, s, 0)
        bv = jnp.where(m, a * b_sh + bv, bv)
        a = jnp.where(m, a * a_sh, a)
        s *= 2
    h = a * per_stream(h_s[...]) + bv
    hl = jnp.concatenate([h[b * Lt + Lt - 1:b * Lt + Lt, :] for b in range(B)], axis=0)
    h_s[...] = hl
    hlast_ref[...] = hl
    y = h * _gelu(gb)
    for b in range(B):
        y_ref[:, b * W:(b + 1) * W] = y[b * Lt:(b + 1) * Lt, :]


def _lru(zl, tail0, h0, p, B, T, Lt, first):
    W = LRU_WIDTH
    kern = functools.partial(_lru_kernel, B=B, Lt=Lt, first=first)
    full = lambda shape: pl.BlockSpec(shape, lambda i: (0,) * len(shape))
    return pl.pallas_call(
        kern,
        grid=(T // Lt,),
        in_specs=[
            pl.BlockSpec((Lt, B * LRU_COLS), lambda i: (i, 0)),
            full((8, B * LRU_COLS)),
            full((B, W)),
            full((CONV_W, W)),
            full((1, W)),
            full((LRU_BLOCKS, LRU_BLOCK, LRU_BLOCK)),
            full((1, W)),
            full((LRU_BLOCKS, LRU_BLOCK, LRU_BLOCK)),
            full((1, W)),
            full((1, W)),
        ],
        out_specs=[
            pl.BlockSpec((Lt, B * W), lambda i: (i, 0)),
            full((B, W)),
        ],
        out_shape=[
            jax.ShapeDtypeStruct((T, B * W), F32),
            jax.ShapeDtypeStruct((B, W), F32),
        ],
        scratch_shapes=[
            pltpu.VMEM((8, B * LRU_COLS), F32),
            pltpu.VMEM((B, W), F32),
        ],
        compiler_params=_cparams(("arbitrary",)),
        name="rglru",
    )(zl, tail0, h0, p["cw"], p["cb"], p["wa"], p["ba"], p["wx"], p["bx"], p["lam"])


def _rwkv_kernel(z_ref, sh0_ref, wt0_ref, mu_ref, w0_ref, w2_ref, a0_ref, a2_ref, g2_ref,
                 kk_ref, ka_ref, rk_ref, lnw_ref, lnb_ref, y_ref, wt_ref, zprev_s, wt_s, *, Tv, NB):
    L = RWKV_CHUNK
    Wd = RWKV_WIDTH
    c = pl.program_id(1)

    @pl.when(c == 0)
    def _():
        zprev_s[...] = sh0_ref[...]
        wt_s[...] = wt0_ref[...]

    row = lax.broadcasted_iota(jnp.int32, (L, 1), 0)
    ri = lax.broadcasted_iota(jnp.int32, (L, L), 0)
    ci = lax.broadcasted_iota(jnp.int32, (L, L), 1)
    strict = ri > ci
    incl = ri >= ci
    tri = jnp.where(incl, 1.0, 0.0).astype(BF16)
    lane = lax.broadcasted_iota(jnp.int32, (1, 2 * RWKV_HEAD), 1)
    m0 = lane < RWKV_HEAD
    bi = lax.broadcasted_iota(jnp.int32, (2 * RWKV_HEAD, 2 * RWKV_HEAD), 0) < RWKV_HEAD
    bj = lax.broadcasted_iota(jnp.int32, (2 * RWKV_HEAD, 2 * RWKV_HEAD), 1) < RWKV_HEAD
    blockdiag = bi == bj
    P = range(RWKV_PAIRS)
    sls = [slice(p * 128, (p + 1) * 128) for p in P]

    def headsum(x):
        s0 = jnp.sum(jnp.where(m0, x, 0.0), axis=-1, keepdims=True)
        s1 = jnp.sum(jnp.where(m0, 0.0, x), axis=-1, keepdims=True)
        return jnp.where(m0, s0, s1)

    pre, g_all, pl_all = {}, {}, {}
    for nb in range(NB):
        z = z_ref[:, nb * RWKV_COLS:(nb + 1) * RWKV_COLS]
        if Tv < L:
            z = jnp.concatenate([z, jnp.zeros((L - Tv, RWKV_COLS), F32)], axis=0)
        zp = jnp.where(row == 0, zprev_s[nb], pltpu.roll(z, 1, 0))
        zprev_s[nb] = z[Tv - 1:Tv, :]
        zs = z + (zp - z) * mu_ref[...]
        r = zs[:, 0:Wd]
        k = zs[:, Wd:2 * Wd]
        v = zs[:, 2 * Wd:3 * Wd]
        zwa = zs[:, 3 * Wd:3 * Wd + LORA_WA]
        zg = zs[:, 3 * Wd + LORA_WA:]
        if Tv < L:
            valid = row < Tv
            r = jnp.where(valid, r, 0.0)
            k = jnp.where(valid, k, 0.0)
            v = jnp.where(valid, v, 0.0)
        w_log = -_softplus(-(w0_ref[...] + _dot(jnp.tanh(zwa).astype(BF16), w2_ref[...]))) - 0.5
        ld = -jnp.exp(w_log)
        if Tv < L:
            ld = jnp.where(valid, ld, 0.0)
        iclr = _sigmoid(a0_ref[...] + _dot(zwa.astype(BF16), a2_ref[...]))
        g_all[nb] = _dot(_sigmoid(zg).astype(BF16), g2_ref[...])
        kkraw = k * kk_ref[...]
        kmod = k * (1.0 + (iclr - 1.0) * ka_ref[...])
        ld_hi = ld.astype(BF16)
        rem = ld - ld_hi.astype(F32)
        ld_mid = rem.astype(BF16)
        ld_lo = (rem - ld_mid.astype(F32)).astype(BF16)
        cum = _dot(tri, ld_hi) + _dot(tri, ld_mid) + _dot(tri, ld_lo)
        cum_l = cum[L - 1:L, :]
        cumm = cum - cum[L // 2 - 1:L // 2, :]
        e_m = jnp.exp(cumm)
        e_mi = jnp.exp(-cumm)
        e_mprev = jnp.exp(cumm - ld)
        e_0 = jnp.exp(cum)
        e_0prev = jnp.exp(cum - ld)
        e_end = jnp.exp(cum_l - cum)
        pl_all[nb] = jnp.exp(cum_l)
        for p in P:
            sl = sls[p]
            r_p, v_p, kmod_p, iclr_p = r[:, sl], v[:, sl], kmod[:, sl], iclr[:, sl]
            kkr = kkraw[:, sl]
            kk = kkr / jnp.maximum(jnp.sqrt(headsum(kkr * kkr)), 1e-12)
            a_p = -kk
            b_p = kk * iclr_p
            at = a_p * e_mprev[:, sl]
            rt = r_p * e_m[:, sl]
            bk = jnp.concatenate([b_p * e_mi[:, sl], kmod_p * e_mi[:, sl]], axis=0).astype(BF16)
            ar0 = jnp.concatenate([a_p * e_0prev[:, sl], r_p * e_0[:, sl]], axis=0).astype(BF16)
            be = jnp.concatenate([b_p * e_end[:, sl], kmod_p * e_end[:, sl]], axis=0).astype(BF16)
            pre[(nb, p)] = dict(r=r_p, v=v_p, kmod=kmod_p, at=at, rt=rt, bk=bk, ar0=ar0, be=be,
                                vb=v_p.astype(BF16))
    pairs = [(nb, p) for nb in range(NB) for p in P]
    wts = {q: wt_s[q[0] * RWKV_PAIRS + q[1]] for q in pairs}
    as0 = {q: _dot_nt(pre[q]["ar0"], wts[q].astype(BF16)) for q in pairs}
    heads = [(q, h) for q in pairs for h in range(2)]
    gms = {}
    for (q, h) in heads:
        mh = m0 if h == 0 else jnp.logical_not(m0)
        arm = jnp.concatenate([jnp.where(mh, pre[q]["at"], 0.0), jnp.where(mh, pre[q]["rt"], 0.0)],
                              axis=0)
        gms[(q, h)] = _dot_nt(arm.astype(BF16), pre[q]["bk"])
    nab, mm, ys = {}, {}, {}
    for (q, h) in heads:
        gm = gms[(q, h)]
        nab[(q, h)] = jnp.where(strict, gm[0:L, 0:L], 0.0).astype(BF16)
        n_ak = jnp.where(strict, gm[0:L, L:2 * L], 0.0)
        mm[(q, h)] = jnp.concatenate([jnp.where(incl, gm[L:2 * L, 0:L], 0.0),
                                      jnp.where(incl, gm[L:2 * L, L:2 * L], 0.0)],
                                     axis=1).astype(BF16)
        ys[(q, h)] = as0[q][0:L] + _dot(n_ak.astype(BF16), pre[q]["vb"])
    pws = dict(nab)
    nlev = int(math.log2(L))
    for lev in range(nlev):
        for hd in heads:
            ys[hd] = ys[hd] + _dot(pws[hd], ys[hd].astype(BF16))
        if lev + 1 < nlev:
            for hd in heads:
                pws[hd] = _dot(pws[hd], pws[hd]).astype(BF16)
    for q in pairs:
        nb, p = q
        sl = sls[p]
        pp = pre[q]
        u_p = jnp.where(m0, ys[(q, 0)], ys[(q, 1)])
        uv = jnp.concatenate([u_p, pp["v"]], axis=0).astype(BF16)
        o0 = as0[q][L:2 * L] + _dot(mm[(q, 0)], uv)
        o1 = as0[q][L:2 * L] + _dot(mm[(q, 1)], uv)
        o = jnp.where(m0, o0, o1)
        wt_new = wts[q] * pl_all[nb][:, sl] + jnp.where(blockdiag, _dot_tn(uv, pp["be"]), 0.0)
        wt_s[nb * RWKV_PAIRS + p] = wt_new
        wt_ref[nb * RWKV_PAIRS + p] = wt_new
        mean = headsum(o) * (1.0 / RWKV_HEAD)
        d = o - mean
        var = headsum(d * d) * (1.0 / RWKV_HEAD)
        on = d * lax.rsqrt(var + GN_EPS) * lnw_ref[:, sl] + lnb_ref[:, sl]
        bonus = headsum(pp["r"] * pp["kmod"] * rk_ref[:, sl]) * pp["v"]
        y = (on + bonus) * g_all[nb][:, sl]
        y_ref[:, nb * Wd + p * 128:nb * Wd + (p + 1) * 128] = y[0:Tv]


def _rwkv(zr, sh0, wt0, p, B, T):
    Tv = min(T, RWKV_CHUNK)
    NB = RWKV_NB
    kern = functools.partial(_rwkv_kernel, Tv=Tv, NB=NB)
    row = lambda n: pl.BlockSpec((1, n), lambda b, c: (0, 0))
    mat = lambda m, n: pl.BlockSpec((m, n), lambda b, c: (0, 0))
    st = pl.BlockSpec((NB * RWKV_PAIRS, 128, 128), lambda b, c: (b, 0, 0))
    return pl.pallas_call(
        kern,
        grid=(B // NB, T // Tv),
        in_specs=[
            pl.BlockSpec((Tv, NB * RWKV_COLS), lambda b, c: (c, b)),
            pl.BlockSpec((NB, 1, RWKV_COLS), lambda b, c: (b, 0, 0)),
            st,
            row(RWKV_COLS), row(RWKV_WIDTH), mat(LORA_WA, RWKV_WIDTH), row(RWKV_WIDTH),
            mat(LORA_WA, RWKV_WIDTH), mat(G_LORA, RWKV_WIDTH),
            row(RWKV_WIDTH), row(RWKV_WIDTH), row(RWKV_WIDTH), row(RWKV_WIDTH), row(RWKV_WIDTH),
        ],
        out_specs=[
            pl.BlockSpec((Tv, NB * RWKV_WIDTH), lambda b, c: (c, b)),
            st,
        ],
        out_shape=[
            jax.ShapeDtypeStruct((T, B * RWKV_WIDTH), F32),
            jax.ShapeDtypeStruct((B * RWKV_PAIRS, 128, 128), F32),
        ],
        scratch_shapes=[
            pltpu.VMEM((NB, 1, RWKV_COLS), F32),
            pltpu.VMEM((NB * RWKV_PAIRS, 128, 128), F32),
        ],
        compiler_params=_cparams(("arbitrary", "arbitrary")),
        name="rwkv7",
    )(zr, sh0, wt0, p["mu"], p["w0"], p["w2"], p["a0"], p["a2"], p["g2"],
      p["kk"], p["ka"], p["rk"], p["lnw"], p["lnb"])


def _outproj_kernel(x_ref, yl_ref, yr_ref, wa_ref, wb_ref, o_ref):
    o_ref[...] = (x_ref[...] + _dot(yl_ref[...].astype(BF16), wa_ref[...])
                  + _dot(yr_ref[...].astype(BF16), wb_ref[...]))


def _outproj(x, yl, yr, wa, wb, tm):
    B, T, D = x.shape
    return pl.pallas_call(
        _outproj_kernel,
        grid=(B, T // tm),
        in_specs=[
            pl.BlockSpec((None, tm, D), lambda b, i: (b, i, 0)),
            pl.BlockSpec((tm, LRU_WIDTH), lambda b, i: (i, b)),
            pl.BlockSpec((tm, RWKV_WIDTH), lambda b, i: (i, b)),
            pl.BlockSpec((LRU_WIDTH, D), lambda b, i: (0, 0)),
            pl.BlockSpec((RWKV_WIDTH, D), lambda b, i: (0, 0)),
        ],
        out_specs=pl.BlockSpec((None, tm, D), lambda b, i: (b, i, 0)),
        out_shape=jax.ShapeDtypeStruct((B, T, D), F32),
        compiler_params=_cparams(("parallel", "parallel")),
        name="outproj",
    )(x, yl, yr, wa, wb)


def _sort16_pairs():
    pairs = []

    def merge(lo, n, r):
        step = 2 * r
        if step < n:
            merge(lo, n, step)
            merge(lo + r, n, step)
            for i in range(lo + r, lo + n - r, step):
                pairs.append((i, i + r))
        else:
            pairs.append((lo, lo + r))

    def sort(lo, n):
        if n > 1:
            sort(lo, n // 2)
            sort(lo + n // 2, n // 2)
            merge(lo, n, 1)

    sort(0, 16)
    return pairs


_SORT16 = _sort16_pairs()


def _merge_top(g, n):
    g = list(g)
    S = g[0].shape[0]
    sub = lax.broadcasted_iota(jnp.int32, g[0].shape, 0).astype(F32)
    rows = []
    for it in range(n):
        m = jnp.max(g[0], axis=0, keepdims=True)
        rows.append(m)
        if it + 1 < n:
            first = jnp.min(jnp.where(g[0] == m, sub, float(S)), axis=0, keepdims=True)
            sel = sub == first
            for v in range(min(n - 1 - it, len(g) - 1)):
                g[v] = jnp.where(sel, g[v + 1], g[v])
            if len(g) <= n - 1 - it:
                g[-1] = jnp.where(sel, -jnp.inf, g[-1])
    return rows


def _top16_rows(x):
    g = [x[8 * v:8 * v + 8, :] for v in range(16)]
    for (i, j) in _SORT16:
        g[i], g[j] = jnp.maximum(g[i], g[j]), jnp.minimum(g[i], g[j])
    return _merge_top(g, PEER_TOPK)


def _gelu2(x):
    c = math.sqrt(2.0 / math.pi)
    inner = x * (c + (c * 0.044715) * (x * x))
    hx = 0.5 * x
    return hx + hx * jnp.tanh(inner)


def _peer_kernel(h_ref, g_ref, wq_ref, keys_ref, u_ref, vt_ref, o_ref,
                 xnt_s, s1_s, s2_s, tau_s, s1x_s, w_s, hid_s, acc_s, *, Tt, Ec):
    j = pl.program_id(1)
    nI = Ec // N_KEYS
    nT = Tt // 128
    Kk = PEER_TOPK
    log2e = 1.0 / math.log(2.0)

    @pl.when(j == 0)
    def _prep():
        xn = _rms(h_ref[...], g_ref[...])
        xnb = xn.astype(BF16)
        xnt_s[...] = xn.T.astype(BF16)
        acc_s[...] = jnp.zeros_like(acc_s)
        sub = lax.broadcasted_iota(jnp.int32, (Kk, Tt), 0)

        def cands(a1, t2):
            return [jnp.where(sub < Kk // (q + 1), a1 + t2[q], -jnp.inf) for q in range(Kk)]

        for h in range(PEER_HEADS):
            q = _dot_nt(wq_ref[h * 2 * PEER_HALF:(h + 1) * 2 * PEER_HALF, :], xnb)
            s1 = _dot(keys_ref[2 * h], q[0:PEER_HALF].astype(BF16))
            s2 = _dot(keys_ref[2 * h + 1], q[PEER_HALF:].astype(BF16))
            t1 = _top16_rows(s1)
            t2 = _top16_rows(s2)
            sc1 = (s1 - t1[0]) * log2e
            sc2 = (s2 - t2[0]) * log2e
            a1 = (jnp.concatenate(t1, axis=0) - t1[0]) * log2e
            a2 = [(t - t2[0]) * log2e for t in t2]
            zsum = jnp.zeros_like(t1[0])
            for b_ in _merge_top(cands(a1, a2), Kk):
                zsum = zsum + jnp.exp2(b_)
            lz = jnp.log2(zsum)
            sc2 = sc2 - lz
            a2 = [t - lz for t in a2]
            tau_s[h:h + 1, :] = _merge_top(cands(a1, a2), Kk)[Kk - 1]
            for tb in range(nT):
                cs = slice(tb * 128, (tb + 1) * 128)
                s1_s[h, tb] = sc1[:, cs]
                s2_s[h, tb] = sc2[:, cs]

    HK = N_KEYS // 2
    IG = 2
    r0d = pl.multiple_of(j * nI, nI)
    for h in range(PEER_HEADS):
        for tb in range(nT):
            s1x_s[h, tb] = s1_s[h, tb, pl.ds(r0d, nI), :]
    hid_s[...] = _dot(u_ref[...], xnt_s[...])
    for tb in range(nT):
        cs = slice(tb * 128, (tb + 1) * 128)
        for half in range(2):
            ks = slice(half * HK, (half + 1) * HK)
            for ig in range(0, nI, IG):
                accs = [jnp.zeros((HK, 128), F32) for _ in range(IG)]
                for h in range(PEER_HEADS):
                    s2h = s2_s[h, tb, ks, :]
                    tau = tau_s[h:h + 1, cs]
                    for ii in range(IG):
                        i1 = ig + ii
                        cval = s1x_s[h, tb, i1:i1 + 1, :] + s2h
                        accs[ii] = accs[ii] + jnp.where(cval >= tau, jnp.exp2(cval), 0.0)
                for ii in range(IG):
                    r0 = (ig + ii) * N_KEYS + half * HK
                    rs = slice(r0, r0 + HK)
                    w_s[rs, cs] = (accs[ii] * _gelu2(hid_s[rs, cs])).astype(BF16)
    acc_s[...] += _dot(vt_ref[...], w_s[...])

    @pl.when(j == pl.num_programs(1) - 1)
    def _fin():
        o_ref[...] = h_ref[...] + acc_s[...].T


def _peer(h, g, wq_t, keys, u, vt, Tt, Ec):
    N, D = h.shape
    nT = Tt // 128
    nI = Ec // N_KEYS
    kern = functools.partial(_peer_kernel, Tt=Tt, Ec=Ec)
    score = pltpu.VMEM((PEER_HEADS, nT, N_KEYS, 128), F32)
    rows = pltpu.VMEM((PEER_HEADS, nT, nI, 128), F32)
    return pl.pallas_call(
        kern,
        grid=(N // Tt, N_EXPERTS // Ec),
        in_specs=[
            pl.BlockSpec((Tt, D), lambda i, j: (i, 0)),
            pl.BlockSpec((1, D), lambda i, j: (0, 0)),
            pl.BlockSpec((PEER_HEADS * 2 * PEER_HALF, D), lambda i, j: (0, 0)),
            pl.BlockSpec((2 * PEER_HEADS, N_KEYS, PEER_HALF), lambda i, j: (0, 0, 0)),
            pl.BlockSpec((Ec, D), lambda i, j: (j, 0)),
            pl.BlockSpec((D, Ec), lambda i, j: (0, j)),
        ],
        out_specs=pl.BlockSpec((Tt, D), lambda i, j: (i, 0)),
        out_shape=jax.ShapeDtypeStruct((N, D), F32),
        scratch_shapes=[
            pltpu.VMEM((D, Tt), BF16),
            score, score,
            pltpu.VMEM((PEER_HEADS, Tt), F32),
            rows,
            pltpu.VMEM((Ec, Tt), BF16),
            pltpu.VMEM((Ec, Tt), F32),
            pltpu.VMEM((D, Tt), F32),
        ],
        compiler_params=_cparams(("parallel", "arbitrary")),
        name="peer",
    )(h, g, wq_t, keys, u, vt)


def _ple_kernel(h_ref, p_ref, g_ref, pw_ref, gw_ref, fg_ref, o_ref, *, final):
    h = h_ref[...]
    gate = _sigmoid(_dot(_rms(h, g_ref[...]).astype(BF16), gw_ref[...]))
    out = h + _dot(p_ref[...].astype(BF16), pw_ref[...]) * gate
    if final:
        out = _rms(out, fg_ref[...])
    o_ref[...] = out


def _ple(h, p, g, pw, gw, fg, tm, final):
    N, D = h.shape
    return pl.pallas_call(
        functools.partial(_ple_kernel, final=final),
        grid=(N // tm,),
        in_specs=[
            pl.BlockSpec((tm, D), lambda i: (i, 0)),
            pl.BlockSpec((tm, D_PLE), lambda i: (i, 0)),
            pl.BlockSpec((1, D), lambda i: (0, 0)),
            pl.BlockSpec((D_PLE, D), lambda i: (0, 0)),
            pl.BlockSpec((D, D), lambda i: (0, 0)),
            pl.BlockSpec((1, D), lambda i: (0, 0)),
        ],
        out_specs=pl.BlockSpec((tm, D), lambda i: (i, 0)),
        out_shape=jax.ShapeDtypeStruct((N, D), F32),
        compiler_params=_cparams(("parallel",)),
        name="ple",
    )(h, p, g, pw, gw, fg)


def _layer_params(i, norm1_g, w_in, conv_w, conv_b, lru_wa, lru_ba, lru_wx, lru_bx, lru_lambda,
                  rwkv_mu, rwkv_w0, rwkv_w2, rwkv_a0, rwkv_a2, rwkv_g2, rwkv_kk, rwkv_ka, rwkv_rk,
                  rwkv_ln_w, rwkv_ln_b, w_out, norm2_g, peer_wq, peer_keys, peer_u, peer_v,
                  norm3_g, ple_w, ple_gate):
    row = lambda a: a.reshape(1, -1)
    wi = w_in[i].astype(BF16)
    zpad = jnp.zeros((LORA_WA // 2, RWKV_WIDTH), BF16)
    return dict(
        g1=row(norm1_g[i]), wl=wi[:, :LRU_COLS], wr=wi[:, LRU_COLS:],
        lru=dict(cw=conv_w[i], cb=row(conv_b[i]), wa=lru_wa[i].astype(BF16), ba=row(lru_ba[i]),
                 wx=lru_wx[i].astype(BF16), bx=row(lru_bx[i]), lam=row(lru_lambda[i])),
        rwkv=dict(mu=row(rwkv_mu[i]), w0=row(rwkv_w0[i]),
                  w2=jnp.concatenate([rwkv_w2[i].astype(BF16), zpad], axis=0), a0=row(rwkv_a0[i]),
                  a2=jnp.concatenate([zpad, rwkv_a2[i].astype(BF16)], axis=0),
                  g2=rwkv_g2[i].astype(BF16), kk=row(rwkv_kk[i]), ka=row(rwkv_ka[i]),
                  rk=row(rwkv_rk[i]), lnw=row(rwkv_ln_w[i]), lnb=row(rwkv_ln_b[i])),
        wo_a=w_out[i, :LRU_WIDTH].astype(BF16), wo_b=w_out[i, LRU_WIDTH:].astype(BF16),
        g2=row(norm2_g[i]), wq_t=peer_wq[i].T.astype(BF16),
        keys=peer_keys[i].reshape(2 * PEER_HEADS, N_KEYS, PEER_HALF).astype(BF16),
        u=peer_u[i].astype(BF16), vt=peer_v[i].T.astype(BF16),
        g3=row(norm3_g[i]), pw=ple_w[i].astype(BF16), gw=ple_gate[i].astype(BF16),
    )


def _pairs_from_state(s):
    B = s.shape[0]
    s = s.reshape(B, RWKV_PAIRS, 2, RWKV_HEAD, RWKV_HEAD)
    z = jnp.zeros_like(s[:, :, 0])
    top = jnp.concatenate([s[:, :, 0], z], axis=-1)
    bot = jnp.concatenate([z, s[:, :, 1]], axis=-1)
    return jnp.concatenate([top, bot], axis=-2)


def _state_from_pairs(w):
    B = w.shape[0]
    w = w.reshape(B, RWKV_PAIRS, 2, RWKV_HEAD, 2, RWKV_HEAD)
    return jnp.stack([w[:, :, 0, :, 0, :], w[:, :, 1, :, 1, :]], axis=2).reshape(
        B, RWKV_HEADS, RWKV_HEAD, RWKV_HEAD)


def _trunk(x, p, conv_st, lru_st, shift_st, wkv_st, first, layers, final_g, tiles):
    B, T, D = x.shape
    depth = len(layers)
    tm, lt = tiles
    tt, ec = PEER_TT, PEER_EC
    h = x
    convs, lrus, shifts, wkvs = [], [], [], []
    for i, lp in enumerate(layers):
        zl, zr = _inproj(h, lp["g1"], lp["wl"], lp["wr"], tm)
        tail0 = jnp.pad(jnp.swapaxes(conv_st[i], 0, 1),
                        ((8 - (CONV_W - 1), 0), (0, 0), (0, LRU_COLS - LRU_WIDTH))).reshape(8, B * LRU_COLS)
        y_lru, h_lru = _lru(zl, tail0, lru_st[i], lp["lru"], B, T, lt, first)
        y_rw, wt = _rwkv(zr, shift_st[i].reshape(B, 1, RWKV_COLS),
                         _pairs_from_state(wkv_st[i]).reshape(B * RWKV_PAIRS, 128, 128), lp["rwkv"], B, T)
        h1 = _outproj(h, y_lru, y_rw, lp["wo_a"], lp["wo_b"], tm)
        h2 = _peer(h1.reshape(B * T, D), lp["g2"], lp["wq_t"], lp["keys"], lp["u"], lp["vt"], tt, ec)
        h3 = _ple(h2, p[i].reshape(B * T, D_PLE), lp["g3"], lp["pw"], lp["gw"], final_g, tt,
                  final=(i == depth - 1))
        h = h3.reshape(B, T, D)
        ztail = zl[T - (CONV_W - 1):].reshape(CONV_W - 1, B, LRU_COLS)
        convs.append(jnp.swapaxes(ztail[:, :, :LRU_WIDTH], 0, 1))
        lrus.append(h_lru)
        shifts.append(zr[T - 1].reshape(B, RWKV_COLS))
        wkvs.append(_state_from_pairs(wt.reshape(B, RWKV_PAIRS, 128, 128)))
    return h, jnp.stack(convs), jnp.stack(lrus), jnp.stack(shifts), jnp.stack(wkvs)


def kernel(x_prompt, x_sample, p_prompt, p_sample, state_conv, state_lru, state_shift, state_wkv, norm1_g, w_in, conv_w, conv_b, lru_wa, lru_ba, lru_wx, lru_bx, lru_lambda, rwkv_mu, rwkv_w0, rwkv_w2, rwkv_a0, rwkv_a2, rwkv_g2, rwkv_kk, rwkv_ka, rwkv_rk, rwkv_ln_w, rwkv_ln_b, w_out, norm2_g, peer_wq, peer_keys, peer_u, peer_v, norm3_g, ple_w, ple_gate, final_g):
    depth = w_in.shape[0]
    layers = [_layer_params(i, norm1_g, w_in, conv_w, conv_b, lru_wa, lru_ba, lru_wx, lru_bx,
                            lru_lambda, rwkv_mu, rwkv_w0, rwkv_w2, rwkv_a0, rwkv_a2, rwkv_g2,
                            rwkv_kk, rwkv_ka, rwkv_rk, rwkv_ln_w, rwkv_ln_b, w_out, norm2_g,
                            peer_wq, peer_keys, peer_u, peer_v, norm3_g, ple_w, ple_gate)
              for i in range(depth)]
    fg = final_g.reshape(1, -1)
    bp = x_prompt.shape[0]
    dt = x_prompt.dtype
    zc = jnp.zeros((depth, bp, CONV_W - 1, LRU_WIDTH), dt)
    zl = jnp.zeros((depth, bp, LRU_WIDTH), dt)
    zs = jnp.zeros((depth, bp, RWKV_COLS), dt)
    zw = jnp.zeros((depth, bp, RWKV_HEADS, RWKV_HEAD, RWKV_HEAD), dt)
    tp = min(512, x_prompt.shape[1])
    ts = min(512, x_sample.shape[1])
    out_p = _trunk(x_prompt, p_prompt, zc, zl, zs, zw, True, layers, fg,
                   (tp, min(32, x_prompt.shape[1])))
    out_s = _trunk(x_sample, p_sample, state_conv, state_lru, state_shift, state_wkv, False,
                   layers, fg, (ts, min(32, x_sample.shape[1])))
    return (out_p[0], out_s[0]) + tuple(out_p[1:]) + tuple(out_s[1:])
```

```python
import functools
import math

import jax
import jax.numpy as jnp
from jax import lax
from jax.experimental import pallas as pl
from jax.experimental.pallas import tpu as pltpu

F32 = jnp.float32
BF16 = jnp.bfloat16

D_MODEL = 1024
D_PLE = 256
LRU_WIDTH = 512
LRU_BLOCKS = 4
LRU_BLOCK = 128
CONV_W = 4
LRU_C = 8.0
RWKV_WIDTH = 512
RWKV_HEAD = 64
RWKV_HEADS = 8
RWKV_PAIRS = RWKV_HEADS // 2
LORA_WA = 128
G_LORA = 128
RWKV_COLS = 3 * RWKV_WIDTH + LORA_WA + G_LORA
LRU_COLS = 2 * LRU_WIDTH
PEER_HEADS = 8
N_KEYS = 128
N_EXPERTS = N_KEYS * N_KEYS
PEER_HALF = 128
PEER_TOPK = 16
PEER_TT = 512
PEER_EC = 1024
EPS = 1e-6
GN_EPS = 64e-5

RWKV_CHUNK = 128
RWKV_NB = 2
VMEM_LIMIT = 56 * 1024 * 1024


def _cparams(sem):
    return pltpu.CompilerParams(dimension_semantics=sem, vmem_limit_bytes=VMEM_LIMIT)


def _rms(x, g):
    return x * lax.rsqrt(jnp.mean(x * x, axis=-1, keepdims=True) + EPS) * g


def _gelu(x):
    c = math.sqrt(2.0 / math.pi)
    return x * (0.5 * (1.0 + jnp.tanh(c * (x + 0.044715 * (x * x * x)))))


def _sigmoid(x):
    return 1.0 / (1.0 + jnp.exp(-x))


def _softplus(x):
    return jnp.maximum(x, 0.0) + jnp.log1p(jnp.exp(-jnp.abs(x)))


def _dot(a, b):
    return jnp.dot(a, b, preferred_element_type=F32)


def _dot_nt(a, b):
    return lax.dot_general(a, b, (((1,), (1,)), ((), ())), preferred_element_type=F32)


def _dot_tn(a, b):
    return lax.dot_general(a, b, (((0,), (0,)), ((), ())), preferred_element_type=F32)


def _inproj_kernel(x_ref, g_ref, wl_ref, wr_ref, zl_ref, zr_ref):
    xn = _rms(x_ref[...], g_ref[...]).astype(BF16)
    zl_ref[...] = _dot(xn, wl_ref[...])
    zr_ref[...] = _dot(xn, wr_ref[...])


def _inproj(x, g, wl, wr, tm):
    B, T, D = x.shape
    return pl.pallas_call(
        _inproj_kernel,
        grid=(B, T // tm),
        in_specs=[
            pl.BlockSpec((None, tm, D), lambda b, i: (b, i, 0)),
            pl.BlockSpec((1, D), lambda b, i: (0, 0)),
            pl.BlockSpec((D, LRU_COLS), lambda b, i: (0, 0)),
            pl.BlockSpec((D, RWKV_COLS), lambda b, i: (0, 0)),
        ],
        out_specs=[
            pl.BlockSpec((tm, LRU_COLS), lambda b, i: (i, b)),
            pl.BlockSpec((tm, RWKV_COLS), lambda b, i: (i, b)),
        ],
        out_shape=[
            jax.ShapeDtypeStruct((T, B * LRU_COLS), F32),
            jax.ShapeDtypeStruct((T, B * RWKV_COLS), F32),
        ],
        compiler_params=_cparams(("parallel", "parallel")),
        name="inproj",
    )(x, g, wl, wr)


def _lru_kernel(z_ref, tail0_ref, h0_ref, cw_ref, cb_ref, wa_ref, ba_ref, wx_ref, bx_ref,
                lam_ref, y_ref, hlast_ref, tail_s, h_s, *, B, Lt, first):
    i = pl.program_id(0)
    W = LRU_WIDTH

    @pl.when(i == 0)
    def _():
        tail_s[...] = tail0_ref[...]
        h_s[...] = h0_ref[...]

    def stack(f):
        return jnp.concatenate([f(b) for b in range(B)], axis=0)

    def per_stream(rows):
        return stack(lambda b: jnp.broadcast_to(rows[b:b + 1, :], (Lt, W)))

    xb = stack(lambda b: z_ref[:, b * LRU_COLS:b * LRU_COLS + W])
    gb = stack(lambda b: z_ref[:, b * LRU_COLS + W:(b + 1) * LRU_COLS])
    tt = lax.broadcasted_iota(jnp.int32, (Lt, 1), 0)
    tt = jnp.concatenate([tt] * B, axis=0)
    xc = cb_ref[...] + xb * cw_ref[CONV_W - 1:CONV_W, :]
    for d in range(1, CONV_W):
        xs = pltpu.roll(xb, d, 0)
        for t0 in range(d):
            r = 8 - d + t0
            prev = stack(lambda b: jnp.broadcast_to(tail_s[r:r + 1, b * LRU_COLS:b * LRU_COLS + W],
                                                    (Lt, W)))
            xs = jnp.where(tt == t0, prev, xs)
        xc = xc + xs * cw_ref[CONV_W - 1 - d:CONV_W - d, :]
    tail_s[...] = z_ref[Lt - 8:Lt, :]

    ra, ix = [], []
    for n in range(LRU_BLOCKS):
        xn = xc[:, n * LRU_BLOCK:(n + 1) * LRU_BLOCK].astype(BF16)
        ra.append(_dot(xn, wa_ref[n]))
        ix.append(_dot(xn, wx_ref[n]))
    r = _sigmoid(jnp.concatenate(ra, axis=1) + ba_ref[...])
    ig = _sigmoid(jnp.concatenate(ix, axis=1) + bx_ref[...])
    log_a = -LRU_C * r * _softplus(-lam_ref[...])
    a = jnp.exp(log_a)
    t = jnp.tanh(log_a)
    mult = jnp.sqrt(-2.0 * t / (1.0 - t))
    if first:
        mult = jnp.where(tt < jnp.where(i == 0, 1, 0), 1.0, mult)
    bv = mult * (ig * xc)
    s = 1
    while s < Lt:
        m = tt >= s
        a_sh = pltpu.roll(a, s, 0)
        b_sh = pltpu.roll(bv, s, 0)
        bv = jnp.where(m, a * b_sh + bv, bv)
        a = jnp.where(m, a * a_sh, a)
        s *= 2
    h = a * per_stream(h_s[...]) + bv
    hl = jnp.concatenate([h[b * Lt + Lt - 1:b * Lt + Lt, :] for b in range(B)], axis=0)
    h_s[...] = hl
    hlast_ref[...] = hl
    y = h * _gelu(gb)
    for b in range(B):
        y_ref[:, b * W:(b + 1) * W] = y[b * Lt:(b + 1) * Lt, :]


def _lru(zl, tail0, h0, p, B, T, Lt, first):
    W = LRU_WIDTH
    kern = functools.partial(_lru_kernel, B=B, Lt=Lt, first=first)
    full = lambda shape: pl.BlockSpec(shape, lambda i: (0,) * len(shape))
    return pl.pallas_call(
        kern,
        grid=(T // Lt,),
        in_specs=[
            pl.BlockSpec((Lt, B * LRU_COLS), lambda i: (i, 0)),
            full((8, B * LRU_COLS)),
            full((B, W)),
            full((CONV_W, W)),
            full((1, W)),
            full((LRU_BLOCKS, LRU_BLOCK, LRU_BLOCK)),
            full((1, W)),
            full((LRU_BLOCKS, LRU_BLOCK, LRU_BLOCK)),
            full((1, W)),
            full((1, W)),
        ],
        out_specs=[
            pl.BlockSpec((Lt, B * W), lambda i: (i, 0)),
            full((B, W)),
        ],
        out_shape=[
            jax.ShapeDtypeStruct((T, B * W), F32),
            jax.ShapeDtypeStruct((B, W), F32),
        ],
        scratch_shapes=[
            pltpu.VMEM((8, B * LRU_COLS), F32),
            pltpu.VMEM((B, W), F32),
        ],
        compiler_params=_cparams(("arbitrary",)),
        name="rglru",
    )(zl, tail0, h0, p["cw"], p["cb"], p["wa"], p["ba"], p["wx"], p["bx"], p["lam"])


def _rwkv_kernel(z_ref, sh0_ref, wt0_ref, mu_ref, w0_ref, w2_ref, a0_ref, a2_ref, g2_ref,
                 kk_ref, ka_ref, rk_ref, lnw_ref, lnb_ref, y_ref, wt_ref, zprev_s, wt_s, *, Tv, NB):
    L = RWKV_CHUNK
    Wd = RWKV_WIDTH
    c = pl.program_id(1)

    @pl.when(c == 0)
    def _():
        zprev_s[...] = sh0_ref[...]
        wt_s[...] = wt0_ref[...]

    row = lax.broadcasted_iota(jnp.int32, (L, 1), 0)
    ri = lax.broadcasted_iota(jnp.int32, (L, L), 0)
    ci = lax.broadcasted_iota(jnp.int32, (L, L), 1)
    strict = ri > ci
    incl = ri >= ci
    tri = jnp.where(incl, 1.0, 0.0).astype(BF16)
    lane = lax.broadcasted_iota(jnp.int32, (1, 2 * RWKV_HEAD), 1)
    m0 = lane < RWKV_HEAD
    bi = lax.broadcasted_iota(jnp.int32, (2 * RWKV_HEAD, 2 * RWKV_HEAD), 0) < RWKV_HEAD
    bj = lax.broadcasted_iota(jnp.int32, (2 * RWKV_HEAD, 2 * RWKV_HEAD), 1) < RWKV_HEAD
    blockdiag = bi == bj
    P = range(RWKV_PAIRS)
    sls = [slice(p * 128, (p + 1) * 128) for p in P]

    def headsum(x):
        s0 = jnp.sum(jnp.where(m0, x, 0.0), axis=-1, keepdims=True)
        s1 = jnp.sum(jnp.where(m0, 0.0, x), axis=-1, keepdims=True)
        return jnp.where(m0, s0, s1)

    pre, g_all, pl_all = {}, {}, {}
    for nb in range(NB):
        z = z_ref[:, nb * RWKV_COLS:(nb + 1) * RWKV_COLS]
        if Tv < L:
            z = jnp.concatenate([z, jnp.zeros((L - Tv, RWKV_COLS), F32)], axis=0)
        zp = jnp.where(row == 0, zprev_s[nb], pltpu.roll(z, 1, 0))
        zprev_s[nb] = z[Tv - 1:Tv, :]
        zs = z + (zp - z) * mu_ref[...]
        r = zs[:, 0:Wd]
        k = zs[:, Wd:2 * Wd]
        v = zs[:, 2 * Wd:3 * Wd]
        zwa = zs[:, 3 * Wd:3 * Wd + LORA_WA]
        zg = zs[:, 3 * Wd + LORA_WA:]
        if Tv < L:
            valid = row < Tv
            r = jnp.where(valid, r, 0.0)
            k = jnp.where(valid, k, 0.0)
            v = jnp.where(valid, v, 0.0)
        w_log = -_softplus(-(w0_ref[...] + _dot(jnp.tanh(zwa).astype(BF16), w2_ref[...]))) - 0.5
        ld = -jnp.exp(w_log)
        if Tv < L:
            ld = jnp.where(valid, ld, 0.0)
        iclr = _sigmoid(a0_ref[...] + _dot(zwa.astype(BF16), a2_ref[...]))
        g_all[nb] = _dot(_sigmoid(zg).astype(BF16), g2_ref[...])
        kkraw = k * kk_ref[...]
        kmod = k * (1.0 + (iclr - 1.0) * ka_ref[...])
        ld_hi = ld.astype(BF16)
        rem = ld - ld_hi.astype(F32)
        ld_mid = rem.astype(BF16)
        ld_lo = (rem - ld_mid.astype(F32)).astype(BF16)
        cum = _dot(tri, ld_hi) + _dot(tri, ld_mid) + _dot(tri, ld_lo)
        cum_l = cum[L - 1:L, :]
        cumm = cum - cum[L // 2 - 1:L // 2, :]
        e_m = jnp.exp(cumm)
        e_mi = jnp.exp(-cumm)
        e_mprev = jnp.exp(cumm - ld)
        e_0 = jnp.exp(cum)
        e_0prev = jnp.exp(cum - ld)
        e_end = jnp.exp(cum_l - cum)
        pl_all[nb] = jnp.exp(cum_l)
        for p in P:
            sl = sls[p]
            r_p, v_p, kmod_p, iclr_p = r[:, sl], v[:, sl], kmod[:, sl], iclr[:, sl]
            kkr = kkraw[:, sl]
            kk = kkr / jnp.maximum(jnp.sqrt(headsum(kkr * kkr)), 1e-12)
            a_p = -kk
            b_p = kk * iclr_p
            at = a_p * e_mprev[:, sl]
            rt = r_p * e_m[:, sl]
            bk = jnp.concatenate([b_p * e_mi[:, sl], kmod_p * e_mi[:, sl]], axis=0).astype(BF16)
            ar0 = jnp.concatenate([a_p * e_0prev[:, sl], r_p * e_0[:, sl]], axis=0).astype(BF16)
            be = jnp.concatenate([b_p * e_end[:, sl], kmod_p * e_end[:, sl]], axis=0).astype(BF16)
            pre[(nb, p)] = dict(r=r_p, v=v_p, kmod=kmod_p, at=at, rt=rt, bk=bk, ar0=ar0, be=be,
                                vb=v_p.astype(BF16))
    pairs = [(nb, p) for nb in range(NB) for p in P]
    wts = {q: wt_s[q[0] * RWKV_PAIRS + q[1]] for q in pairs}
    as0 = {q: _dot_nt(pre[q]["ar0"], wts[q].astype(BF16)) for q in pairs}
    heads = [(q, h) for q in pairs for h in range(2)]
    gms = {}
    for (q, h) in heads:
        mh = m0 if h == 0 else jnp.logical_not(m0)
        arm = jnp.concatenate([jnp.where(mh, pre[q]["at"], 0.0), jnp.where(mh, pre[q]["rt"], 0.0)],
                              axis=0)
        gms[(q, h)] = _dot_nt(arm.astype(BF16), pre[q]["bk"])
    nab, mm, ys = {}, {}, {}
    for (q, h) in heads:
        gm = gms[(q, h)]
        nab[(q, h)] = jnp.where(strict, gm[0:L, 0:L], 0.0).astype(BF16)
        n_ak = jnp.where(strict, gm[0:L, L:2 * L], 0.0)
        mm[(q, h)] = jnp.concatenate([jnp.where(incl, gm[L:2 * L, 0:L], 0.0),
                                      jnp.where(incl, gm[L:2 * L, L:2 * L], 0.0)],
                                     axis=1).astype(BF16)
        ys[(q, h)] = as0[q][0:L] + _dot(n_ak.astype(BF16), pre[q]["vb"])
    pws = dict(nab)
    nlev = int(math.log2(L))
    for lev in range(nlev):
        for hd in heads:
            ys[hd] = ys[hd] + _dot(pws[hd], ys[hd].astype(BF16))
        if lev + 1 < nlev:
            for hd in heads:
                pws[hd] = _dot(pws[hd], pws[hd]).astype(BF16)
    for q in pairs:
        nb, p = q
        sl = sls[p]
        pp = pre[q]
        u_p = jnp.where(m0, ys[(q, 0)], ys[(q, 1)])
        uv = jnp.concatenate([u_p, pp["v"]], axis=0).astype(BF16)
        o0 = as0[q][L:2 * L] + _dot(mm[(q, 0)], uv)
        o1 = as0[q][L:2 * L] + _dot(mm[(q, 1)], uv)
        o = jnp.where(m0, o0, o1)
        wt_new = wts[q] * pl_all[nb][:, sl] + jnp.where(blockdiag, _dot_tn(uv, pp["be"]), 0.0)
        wt_s[nb * RWKV_PAIRS + p] = wt_new
        wt_ref[nb * RWKV_PAIRS + p] = wt_new
        mean = headsum(o) * (1.0 / RWKV_HEAD)
        d = o - mean
        var = headsum(d * d) * (1.0 / RWKV_HEAD)
        on = d * lax.rsqrt(var + GN_EPS) * lnw_ref[:, sl] + lnb_ref[:, sl]
        bonus = headsum(pp["r"] * pp["kmod"] * rk_ref[:, sl]) * pp["v"]
        y = (on + bonus) * g_all[nb][:, sl]
        y_ref[:, nb * Wd + p * 128:nb * Wd + (p + 1) * 128] = y[0:Tv]


def _rwkv(zr, sh0, wt0, p, B, T):
    Tv = min(T, RWKV_CHUNK)
    NB = RWKV_NB
    kern = functools.partial(_rwkv_kernel, Tv=Tv, NB=NB)
    row = lambda n: pl.BlockSpec((1, n), lambda b, c: (0, 0))
    mat = lambda m, n: pl.BlockSpec((m, n), lambda b, c: (0, 0))
    st = pl.BlockSpec((NB * RWKV_PAIRS, 128, 128), lambda b, c: (b, 0, 0))
    return pl.pallas_call(
        kern,
        grid=(B // NB, T // Tv),
        in_specs=[
            pl.BlockSpec((Tv, NB * RWKV_COLS), lambda b, c: (c, b)),
            pl.BlockSpec((NB, 1, RWKV_COLS), lambda b, c: (b, 0, 0)),
            st,
            row(RWKV_COLS), row(RWKV_WIDTH), mat(LORA_WA, RWKV_WIDTH), row(RWKV_WIDTH),
            mat(LORA_WA, RWKV_WIDTH), mat(G_LORA, RWKV_WIDTH),
            row(RWKV_WIDTH), row(RWKV_WIDTH), row(RWKV_WIDTH), row(RWKV_WIDTH), row(RWKV_WIDTH),
        ],
        out_specs=[
            pl.BlockSpec((Tv, NB * RWKV_WIDTH), lambda b, c: (c, b)),
            st,
        ],
        out_shape=[
            jax.ShapeDtypeStruct((T, B * RWKV_WIDTH), F32),
            jax.ShapeDtypeStruct((B * RWKV_PAIRS, 128, 128), F32),
        ],
        scratch_shapes=[
            pltpu.VMEM((NB, 1, RWKV_COLS), F32),
            pltpu.VMEM((NB * RWKV_PAIRS, 128, 128), F32),
        ],
        compiler_params=_cparams(("arbitrary", "arbitrary")),
        name="rwkv7",
    )(zr, sh0, wt0, p["mu"], p["w0"], p["w2"], p["a0"], p["a2"], p["g2"],
      p["kk"], p["ka"], p["rk"], p["lnw"], p["lnb"])


def _outproj_kernel(x_ref, yl_ref, yr_ref, wa_ref, wb_ref, o_ref):
    o_ref[...] = (x_ref[...] + _dot(yl_ref[...].astype(BF16), wa_ref[...])
                  + _dot(yr_ref[...].astype(BF16), wb_ref[...]))


def _outproj(x, yl, yr, wa, wb, tm):
    B, T, D = x.shape
    return pl.pallas_call(
        _outproj_kernel,
        grid=(B, T // tm),
        in_specs=[
            pl.BlockSpec((None, tm, D), lambda b, i: (b, i, 0)),
            pl.BlockSpec((tm, LRU_WIDTH), lambda b, i: (i, b)),
            pl.BlockSpec((tm, RWKV_WIDTH), lambda b, i: (i, b)),
            pl.BlockSpec((LRU_WIDTH, D), lambda b, i: (0, 0)),
            pl.BlockSpec((RWKV_WIDTH, D), lambda b, i: (0, 0)),
        ],
        out_specs=pl.BlockSpec((None, tm, D), lambda b, i: (b, i, 0)),
        out_shape=jax.ShapeDtypeStruct((B, T, D), F32),
        compiler_params=_cparams(("parallel", "parallel")),
        name="outproj",
    )(x, yl, yr, wa, wb)


def _sort16_pairs():
    pairs = []

    def merge(lo, n, r):
        step = 2 * r
        if step < n:
            merge(lo, n, step)
            merge(lo + r, n, step)
            for i in range(lo + r, lo + n - r, step):
                pairs.append((i, i + r))
        else:
            pairs.append((lo, lo + r))

    def sort(lo, n):
        if n > 1:
            sort(lo, n // 2)
            sort(lo + n // 2, n // 2)
            merge(lo, n, 1)

    sort(0, 16)
    return pairs


_SORT16 = _sort16_pairs()


def _merge_top(g, n):
    g = list(g)
    S = g[0].shape[0]
    sub = lax.broadcasted_iota(jnp.int32, g[0].shape, 0).astype(F32)
    rows = []
    for it in range(n):
        m = jnp.max(g[0], axis=0, keepdims=True)
        rows.append(m)
        if it + 1 < n:
            first = jnp.min(jnp.where(g[0] == m, sub, float(S)), axis=0, keepdims=True)
            sel = sub == first
            for v in range(min(n - 1 - it, len(g) - 1)):
                g[v] = jnp.where(sel, g[v + 1], g[v])
            if len(g) <= n - 1 - it:
                g[-1] = jnp.where(sel, -jnp.inf, g[-1])
    return rows


def _top16_rows(x):
    g = [x[8 * v:8 * v + 8, :] for v in range(16)]
    for (i, j) in _SORT16:
        g[i], g[j] = jnp.maximum(g[i], g[j]), jnp.minimum(g[i], g[j])
    return _merge_top(g, PEER_TOPK)


def _gelu2(x):
    c = math.sqrt(2.0 / math.pi)
    inner = x * (c + (c * 0.044715) * (x * x))
    hx = 0.5 * x
    return hx + hx * jnp.tanh(inner)


def _peer_kernel(h_ref, g_ref, wq_ref, keys_ref, u_ref, vt_ref, o_ref,
                 xnt_s, s1_s, s2_s, tau_s, s1x_s, w_s, hid_s, acc_s, *, Tt, Ec):
    j = pl.program_id(1)
    nI = Ec // N_KEYS
    nT = Tt // 128
    Kk = PEER_TOPK
    log2e = 1.0 / math.log(2.0)

    @pl.when(j == 0)
    def _prep():
        xn = _rms(h_ref[...], g_ref[...])
        xnb = xn.astype(BF16)
        xnt_s[...] = xn.T.astype(BF16)
        acc_s[...] = jnp.zeros_like(acc_s)
        sub = lax.broadcasted_iota(jnp.int32, (Kk, Tt), 0)

        def cands(a1, t2):
            return [jnp.where(sub < Kk // (q + 1), a1 + t2[q], -jnp.inf) for q in range(Kk)]

        for h in range(PEER_HEADS):
            q = _dot_nt(wq_ref[h * 2 * PEER_HALF:(h + 1) * 2 * PEER_HALF, :], xnb)
            s1 = _dot(keys_ref[2 * h], q[0:PEER_HALF].astype(BF16))
            s2 = _dot(keys_ref[2 * h + 1], q[PEER_HALF:].astype(BF16))
            t1 = _top16_rows(s1)
            t2 = _top16_rows(s2)
            sc1 = (s1 - t1[0]) * log2e
            sc2 = (s2 - t2[0]) * log2e
            a1 = (jnp.concatenate(t1, axis=0) - t1[0]) * log2e
            a2 = [(t - t2[0]) * log2e for t in t2]
            zsum = jnp.zeros_like(t1[0])
            for b_ in _merge_top(cands(a1, a2), Kk):
                zsum = zsum + jnp.exp2(b_)
            lz = jnp.log2(zsum)
            sc2 = sc2 - lz
            a2 = [t - lz for t in a2]
            tau_s[h:h + 1, :] = _merge_top(cands(a1, a2), Kk)[Kk - 1]
            for tb in range(nT):
                cs = slice(tb * 128, (tb + 1) * 128)
                s1_s[h, tb] = sc1[:, cs]
                s2_s[h, tb] = sc2[:, cs]

    HK = N_KEYS // 2
    IG = 2
    r0d = pl.multiple_of(j * nI, nI)
    for h in range(PEER_HEADS):
        for tb in range(nT):
            s1x_s[h, tb] = s1_s[h, tb, pl.ds(r0d, nI), :]
    hid_s[...] = _dot(u_ref[...], xnt_s[...])
    for tb in range(nT):
        cs = slice(tb * 128, (tb + 1) * 128)
        for half in range(2):
            ks = slice(half * HK, (half + 1) * HK)
            for ig in range(0, nI, IG):
                accs = [jnp.zeros((HK, 128), F32) for _ in range(IG)]
                for h in range(PEER_HEADS):
                    s2h = s2_s[h, tb, ks, :]
                    tau = tau_s[h:h + 1, cs]
                    for ii in range(IG):
                        i1 = ig + ii
                        cval = s1x_s[h, tb, i1:i1 + 1, :] + s2h
                        accs[ii] = accs[ii] + jnp.where(cval >= tau, jnp.exp2(cval), 0.0)
                for ii in range(IG):
                    r0 = (ig + ii) * N_KEYS + half * HK
                    rs = slice(r0, r0 + HK)
                    w_s[rs, cs] = accs[ii].astype(BF16) * _gelu2(hid_s[rs, cs].astype(BF16))
    acc_s[...] += _dot(vt_ref[...], w_s[...])

    @pl.when(j == pl.num_programs(1) - 1)
    def _fin():
        o_ref[...] = h_ref[...] + acc_s[...].T


def _peer(h, g, wq_t, keys, u, vt, Tt, Ec):
    N, D = h.shape
    nT = Tt // 128
    nI = Ec // N_KEYS
    kern = functools.partial(_peer_kernel, Tt=Tt, Ec=Ec)
    score = pltpu.VMEM((PEER_HEADS, nT, N_KEYS, 128), F32)
    rows = pltpu.VMEM((PEER_HEADS, nT, nI, 128), F32)
    return pl.pallas_call(
        kern,
        grid=(N // Tt, N_EXPERTS // Ec),
        in_specs=[
            pl.BlockSpec((Tt, D), lambda i, j: (i, 0)),
            pl.BlockSpec((1, D), lambda i, j: (0, 0)),
            pl.BlockSpec((PEER_HEADS * 2 * PEER_HALF, D), lambda i, j: (0, 0)),
            pl.BlockSpec((2 * PEER_HEADS, N_KEYS, PEER_HALF), lambda i, j: (0, 0, 0)),
            pl.BlockSpec((Ec, D), lambda i, j: (j, 0)),
            pl.BlockSpec((D, Ec), lambda i, j: (0, j)),
        ],
        out_specs=pl.BlockSpec((Tt, D), lambda i, j: (i, 0)),
        out_shape=jax.ShapeDtypeStruct((N, D), F32),
        scratch_shapes=[
            pltpu.VMEM((D, Tt), BF16),
            score, score,
            pltpu.VMEM((PEER_HEADS, Tt), F32),
            rows,
            pltpu.VMEM((Ec, Tt), BF16),
            pltpu.VMEM((Ec, Tt), F32),
            pltpu.VMEM((D, Tt), F32),
        ],
        compiler_params=_cparams(("parallel", "arbitrary")),
        name="peer",
    )(h, g, wq_t, keys, u, vt)


def _ple_kernel(h_ref, p_ref, g_ref, pw_ref, gw_ref, fg_ref, o_ref, *, final):
    h = h_ref[...]
    gate = _sigmoid(_dot(_rms(h, g_ref[...]).astype(BF16), gw_ref[...]))
    out = h + _dot(p_ref[...].astype(BF16), pw_ref[...]) * gate
    if final:
        out = _rms(out, fg_ref[...])
    o_ref[...] = out


def _ple(h, p, g, pw, gw, fg, tm, final):
    N, D = h.shape
    return pl.pallas_call(
        functools.partial(_ple_kernel, final=final),
        grid=(N // tm,),
        in_specs=[
            pl.BlockSpec((tm, D), lambda i: (i, 0)),
            pl.BlockSpec((tm, D_PLE), lambda i: (i, 0)),
            pl.BlockSpec((1, D), lambda i: (0, 0)),
            pl.BlockSpec((D_PLE, D), lambda i: (0, 0)),
            pl.BlockSpec((D, D), lambda i: (0, 0)),
            pl.BlockSpec((1, D), lambda i: (0, 0)),
        ],
        out_specs=pl.BlockSpec((tm, D), lambda i: (i, 0)),
        out_shape=jax.ShapeDtypeStruct((N, D), F32),
        compiler_params=_cparams(("parallel",)),
        name="ple",
    )(h, p, g, pw, gw, fg)


def _layer_params(i, norm1_g, w_in, conv_w, conv_b, lru_wa, lru_ba, lru_wx, lru_bx, lru_lambda,
                  rwkv_mu, rwkv_w0, rwkv_w2, rwkv_a0, rwkv_a2, rwkv_g2, rwkv_kk, rwkv_ka, rwkv_rk,
                  rwkv_ln_w, rwkv_ln_b, w_out, norm2_g, peer_wq, peer_keys, peer_u, peer_v,
                  norm3_g, ple_w, ple_gate):
    row = lambda a: a.reshape(1, -1)
    wi = w_in[i].astype(BF16)
    zpad = jnp.zeros((LORA_WA // 2, RWKV_WIDTH), BF16)
    return dict(
        g1=row(norm1_g[i]), wl=wi[:, :LRU_COLS], wr=wi[:, LRU_COLS:],
        lru=dict(cw=conv_w[i], cb=row(conv_b[i]), wa=lru_wa[i].astype(BF16), ba=row(lru_ba[i]),
                 wx=lru_wx[i].astype(BF16), bx=row(lru_bx[i]), lam=row(lru_lambda[i])),
        rwkv=dict(mu=row(rwkv_mu[i]), w0=row(rwkv_w0[i]),
                  w2=jnp.concatenate([rwkv_w2[i].astype(BF16), zpad], axis=0), a0=row(rwkv_a0[i]),
                  a2=jnp.concatenate([zpad, rwkv_a2[i].astype(BF16)], axis=0),
                  g2=rwkv_g2[i].astype(BF16), kk=row(rwkv_kk[i]), ka=row(rwkv_ka[i]),
                  rk=row(rwkv_rk[i]), lnw=row(rwkv_ln_w[i]), lnb=row(rwkv_ln_b[i])),
        wo_a=w_out[i, :LRU_WIDTH].astype(BF16), wo_b=w_out[i, LRU_WIDTH:].astype(BF16),
        g2=row(norm2_g[i]), wq_t=peer_wq[i].T.astype(BF16),
        keys=peer_keys[i].reshape(2 * PEER_HEADS, N_KEYS, PEER_HALF).astype(BF16),
        u=peer_u[i].astype(BF16), vt=peer_v[i].T.astype(BF16),
        g3=row(norm3_g[i]), pw=ple_w[i].astype(BF16), gw=ple_gate[i].astype(BF16),
    )


def _pairs_from_state(s):
    B = s.shape[0]
    s = s.reshape(B, RWKV_PAIRS, 2, RWKV_HEAD, RWKV_HEAD)
    z = jnp.zeros_like(s[:, :, 0])
    top = jnp.concatenate([s[:, :, 0], z], axis=-1)
    bot = jnp.concatenate([z, s[:, :, 1]], axis=-1)
    return jnp.concatenate([top, bot], axis=-2)


def _state_from_pairs(w):
    B = w.shape[0]
    w = w.reshape(B, RWKV_PAIRS, 2, RWKV_HEAD, 2, RWKV_HEAD)
    return jnp.stack([w[:, :, 0, :, 0, :], w[:, :, 1, :, 1, :]], axis=2).reshape(
        B, RWKV_HEADS, RWKV_HEAD, RWKV_HEAD)


def _trunk(x, p, conv_st, lru_st, shift_st, wkv_st, first, layers, final_g, tiles):
    B, T, D = x.shape
    depth = len(layers)
    tm, lt = tiles
    tt, ec = PEER_TT, PEER_EC
    h = x
    convs, lrus, shifts, wkvs = [], [], [], []
    for i, lp in enumerate(layers):
        zl, zr = _inproj(h, lp["g1"], lp["wl"], lp["wr"], tm)
        tail0 = jnp.pad(jnp.swapaxes(conv_st[i], 0, 1),
                        ((8 - (CONV_W - 1), 0), (0, 0), (0, LRU_COLS - LRU_WIDTH))).reshape(8, B * LRU_COLS)
        y_lru, h_lru = _lru(zl, tail0, lru_st[i], lp["lru"], B, T, lt, first)
        y_rw, wt = _rwkv(zr, shift_st[i].reshape(B, 1, RWKV_COLS),
                         _pairs_from_state(wkv_st[i]).reshape(B * RWKV_PAIRS, 128, 128), lp["rwkv"], B, T)
        h1 = _outproj(h, y_lru, y_rw, lp["wo_a"], lp["wo_b"], tm)
        h2 = _peer(h1.reshape(B * T, D), lp["g2"], lp["wq_t"], lp["keys"], lp["u"], lp["vt"], tt, ec)
        h3 = _ple(h2, p[i].reshape(B * T, D_PLE), lp["g3"], lp["pw"], lp["gw"], final_g, tt,
                  final=(i == depth - 1))
        h = h3.reshape(B, T, D)
        ztail = zl[T - (CONV_W - 1):].reshape(CONV_W - 1, B, LRU_COLS)
        convs.append(jnp.swapaxes(ztail[:, :, :LRU_WIDTH], 0, 1))
        lrus.append(h_lru)
        shifts.append(zr[T - 1].reshape(B, RWKV_COLS))
        wkvs.append(_state_from_pairs(wt.reshape(B, RWKV_PAIRS, 128, 128)))
    return h, jnp.stack(convs), jnp.stack(lrus), jnp.stack(shifts), jnp.stack(wkvs)


def kernel(x_prompt, x_sample, p_prompt, p_sample, state_conv, state_lru, state_shift, state_wkv, norm1_g, w_in, conv_w, conv_b, lru_wa, lru_ba, lru_wx, lru_bx, lru_lambda, rwkv_mu, rwkv_w0, rwkv_w2, rwkv_a0, rwkv_a2, rwkv_g2, rwkv_kk, rwkv_ka, rwkv_rk, rwkv_ln_w, rwkv_ln_b, w_out, norm2_g, peer_wq, peer_keys, peer_u, peer_v, norm3_g, ple_w, ple_gate, final_g):
    depth = w_in.shape[0]
    layers = [_layer_params(i, norm1_g, w_in, conv_w, conv_b, lru_wa, lru_ba, lru_wx, lru_bx,
                            lru_lambda, rwkv_mu, rwkv_w0, rwkv_w2, rwkv_a0, rwkv_a2, rwkv_g2,
                            rwkv_kk, rwkv_ka, rwkv_rk, rwkv_ln_w, rwkv_ln_b, w_out, norm2_g,
                            peer_wq, peer_keys, peer_u, peer_v, norm3_g, ple_w, ple_gate)
              for i in range(depth)]
    fg = final_g.reshape(1, -1)
    bp = x_prompt.shape[0]
    dt = x_prompt.dtype
    zc = jnp.zeros((depth, bp, CONV_W - 1, LRU_WIDTH), dt)
    zl = jnp.zeros((depth, bp, LRU_WIDTH), dt)
    zs = jnp.zeros((depth, bp, RWKV_COLS), dt)
    zw = jnp.zeros((depth, bp, RWKV_HEADS, RWKV_HEAD, RWKV_HEAD), dt)
    tp = min(512, x_prompt.shape[1])
    ts = min(512, x_sample.shape[1])
    out_p = _trunk(x_prompt, p_prompt, zc, zl, zs, zw, True, layers, fg,
                   (tp, min(32, x_prompt.shape[1])))
    out_s = _trunk(x_sample, p_sample, state_conv, state_lru, state_shift, state_wkv, False,
                   layers, fg, (ts, min(32, x_sample.shape[1])))
    return (out_p[0], out_s[0]) + tuple(out_p[1:]) + tuple(out_s[1:])
```

```python
import functools
import math

import jax
import jax.numpy as jnp
from jax import lax
from jax.experimental import pallas as pl
from jax.experimental.pallas import tpu as pltpu

F32 = jnp.float32
BF16 = jnp.bfloat16

D_MODEL = 1024
D_PLE = 256
LRU_WIDTH = 512
LRU_BLOCKS = 4
LRU_BLOCK = 128
CONV_W = 4
LRU_C = 8.0
RWKV_WIDTH = 512
RWKV_HEAD = 64
RWKV_HEADS = 8
RWKV_PAIRS = RWKV_HEADS // 2
LORA_WA = 128
G_LORA = 128
RWKV_COLS = 3 * RWKV_WIDTH + LORA_WA + G_LORA
LRU_COLS = 2 * LRU_WIDTH
PEER_HEADS = 8
N_KEYS = 128
N_EXPERTS = N_KEYS * N_KEYS
PEER_HALF = 128
PEER_TOPK = 16
PEER_TT = 512
PEER_EC = 1024
EPS = 1e-6
GN_EPS = 64e-5

RWKV_CHUNK = 128
RWKV_NB = 4
VMEM_LIMIT = 56 * 1024 * 1024


def _cparams(sem):
    return pltpu.CompilerParams(dimension_semantics=sem, vmem_limit_bytes=VMEM_LIMIT)


def _rms(x, g):
    return x * lax.rsqrt(jnp.mean(x * x, axis=-1, keepdims=True) + EPS) * g


def _gelu(x):
    c = math.sqrt(2.0 / math.pi)
    return x * (0.5 * (1.0 + jnp.tanh(c * (x + 0.044715 * (x * x * x)))))


def _sigmoid(x):
    return 1.0 / (1.0 + jnp.exp(-x))


def _softplus(x):
    return jnp.maximum(x, 0.0) + jnp.log1p(jnp.exp(-jnp.abs(x)))


def _dot(a, b):
    return jnp.dot(a, b, preferred_element_type=F32)


def _dot_nt(a, b):
    return lax.dot_general(a, b, (((1,), (1,)), ((), ())), preferred_element_type=F32)


def _dot_tn(a, b):
    return lax.dot_general(a, b, (((0,), (0,)), ((), ())), preferred_element_type=F32)


def _inproj_kernel(x_ref, g_ref, wl_ref, wr_ref, zl_ref, zr_ref):
    xn = _rms(x_ref[...], g_ref[...]).astype(BF16)
    zl_ref[...] = _dot(xn, wl_ref[...])
    zr_ref[...] = _dot(xn, wr_ref[...])


def _inproj(x, g, wl, wr, tm):
    B, T, D = x.shape
    return pl.pallas_call(
        _inproj_kernel,
        grid=(B, T // tm),
        in_specs=[
            pl.BlockSpec((None, tm, D), lambda b, i: (b, i, 0)),
            pl.BlockSpec((1, D), lambda b, i: (0, 0)),
            pl.BlockSpec((D, LRU_COLS), lambda b, i: (0, 0)),
            pl.BlockSpec((D, RWKV_COLS), lambda b, i: (0, 0)),
        ],
        out_specs=[
            pl.BlockSpec((tm, LRU_COLS), lambda b, i: (i, b)),
            pl.BlockSpec((tm, RWKV_COLS), lambda b, i: (i, b)),
        ],
        out_shape=[
            jax.ShapeDtypeStruct((T, B * LRU_COLS), F32),
            jax.ShapeDtypeStruct((T, B * RWKV_COLS), F32),
        ],
        compiler_params=_cparams(("parallel", "parallel")),
        name="inproj",
    )(x, g, wl, wr)


def _lru_kernel(z_ref, tail0_ref, h0_ref, cw_ref, cb_ref, wa_ref, ba_ref, wx_ref, bx_ref,
                lam_ref, y_ref, hlast_ref, tail_s, h_s, *, B, Lt, first):
    i = pl.program_id(0)
    W = LRU_WIDTH

    @pl.when(i == 0)
    def _():
        tail_s[...] = tail0_ref[...]
        h_s[...] = h0_ref[...]

    def stack(f):
        return jnp.concatenate([f(b) for b in range(B)], axis=0)

    def per_stream(rows):
        return stack(lambda b: jnp.broadcast_to(rows[b:b + 1, :], (Lt, W)))

    xb = stack(lambda b: z_ref[:, b * LRU_COLS:b * LRU_COLS + W])
    gb = stack(lambda b: z_ref[:, b * LRU_COLS + W:(b + 1) * LRU_COLS])
    tt = lax.broadcasted_iota(jnp.int32, (Lt, 1), 0)
    tt = jnp.concatenate([tt] * B, axis=0)
    xc = cb_ref[...] + xb * cw_ref[CONV_W - 1:CONV_W, :]
    for d in range(1, CONV_W):
        xs = pltpu.roll(xb, d, 0)
        for t0 in range(d):
            r = 8 - d + t0
            prev = stack(lambda b: jnp.broadcast_to(tail_s[r:r + 1, b * LRU_COLS:b * LRU_COLS + W],
                                                    (Lt, W)))
            xs = jnp.where(tt == t0, prev, xs)
        xc = xc + xs * cw_ref[CONV_W - 1 - d:CONV_W - d, :]
    tail_s[...] = z_ref[Lt - 8:Lt, :]

    ra, ix = [], []
    for n in range(LRU_BLOCKS):
        xn = xc[:, n * LRU_BLOCK:(n + 1) * LRU_BLOCK].astype(BF16)
        ra.append(_dot(xn, wa_ref[n]))
        ix.append(_dot(xn, wx_ref[n]))
    r = _sigmoid(jnp.concatenate(ra, axis=1) + ba_ref[...])
    ig = _sigmoid(jnp.concatenate(ix, axis=1) + bx_ref[...])
    log_a = -LRU_C * r * _softplus(-lam_ref[...])
    a = jnp.exp(log_a)
    t = jnp.tanh(log_a)
    mult = jnp.sqrt(-2.0 * t / (1.0 - t))
    if first:
        mult = jnp.where(tt < jnp.where(i == 0, 1, 0), 1.0, mult)
    bv = mult * (ig * xc)
    s = 1
    while s < Lt:
        m = tt >= s
        a_sh = pltpu.roll(a, s, 0)
        b_sh = pltpu.roll(bv, s, 0)
        bv = jnp.where(m, a * b_sh + bv, bv)
        a = jnp.where(m, a * a_sh, a)
        s *= 2
    h = a * per_stream(h_s[...]) + bv
    hl = jnp.concatenate([h[b * Lt + Lt - 1:b * Lt + Lt, :] for b in range(B)], axis=0)
    h_s[...] = hl
    hlast_ref[...] = hl
    y = h * _gelu(gb)
    for b in range(B):
        y_ref[:, b * W:(b + 1) * W] = y[b * Lt:(b + 1) * Lt, :]


def _lru(zl, tail0, h0, p, B, T, Lt, first):
    W = LRU_WIDTH
    kern = functools.partial(_lru_kernel, B=B, Lt=Lt, first=first)
    full = lambda shape: pl.BlockSpec(shape, lambda i: (0,) * len(shape))
    return pl.pallas_call(
        kern,
        grid=(T // Lt,),
        in_specs=[
            pl.BlockSpec((Lt, B * LRU_COLS), lambda i: (i, 0)),
            full((8, B * LRU_COLS)),
            full((B, W)),
            full((CONV_W, W)),
            full((1, W)),
            full((LRU_BLOCKS, LRU_BLOCK, LRU_BLOCK)),
            full((1, W)),
            full((LRU_BLOCKS, LRU_BLOCK, LRU_BLOCK)),
            full((1, W)),
            full((1, W)),
        ],
        out_specs=[
            pl.BlockSpec((Lt, B * W), lambda i: (i, 0)),
            full((B, W)),
        ],
        out_shape=[
            jax.ShapeDtypeStruct((T, B * W), F32),
            jax.ShapeDtypeStruct((B, W), F32),
        ],
        scratch_shapes=[
            pltpu.VMEM((8, B * LRU_COLS), F32),
            pltpu.VMEM((B, W), F32),
        ],
        compiler_params=_cparams(("arbitrary",)),
        name="rglru",
    )(zl, tail0, h0, p["cw"], p["cb"], p["wa"], p["ba"], p["wx"], p["bx"], p["lam"])


def _rwkv_kernel(z_ref, sh0_ref, wt0_ref, mu_ref, w0_ref, w2_ref, a0_ref, a2_ref, g2_ref,
                 kk_ref, ka_ref, rk_ref, lnw_ref, lnb_ref, y_ref, wt_ref, zprev_s, wt_s, *, Tv, NB):
    L = RWKV_CHUNK
    Wd = RWKV_WIDTH
    c = pl.program_id(1)

    @pl.when(c == 0)
    def _():
        zprev_s[...] = sh0_ref[...]
        wt_s[...] = wt0_ref[...]

    row = lax.broadcasted_iota(jnp.int32, (L, 1), 0)
    ri = lax.broadcasted_iota(jnp.int32, (L, L), 0)
    ci = lax.broadcasted_iota(jnp.int32, (L, L), 1)
    strict = ri > ci
    incl = ri >= ci
    tri = jnp.where(incl, 1.0, 0.0).astype(BF16)
    lane = lax.broadcasted_iota(jnp.int32, (1, 2 * RWKV_HEAD), 1)
    m0 = lane < RWKV_HEAD
    bi = lax.broadcasted_iota(jnp.int32, (2 * RWKV_HEAD, 2 * RWKV_HEAD), 0) < RWKV_HEAD
    bj = lax.broadcasted_iota(jnp.int32, (2 * RWKV_HEAD, 2 * RWKV_HEAD), 1) < RWKV_HEAD
    blockdiag = bi == bj
    P = range(RWKV_PAIRS)
    sls = [slice(p * 128, (p + 1) * 128) for p in P]

    def headsum(x):
        s0 = jnp.sum(jnp.where(m0, x, 0.0), axis=-1, keepdims=True)
        s1 = jnp.sum(jnp.where(m0, 0.0, x), axis=-1, keepdims=True)
        return jnp.where(m0, s0, s1)

    pre, g_all, pl_all = {}, {}, {}
    for nb in range(NB):
        z = z_ref[:, nb * RWKV_COLS:(nb + 1) * RWKV_COLS]
        if Tv < L:
            z = jnp.concatenate([z, jnp.zeros((L - Tv, RWKV_COLS), F32)], axis=0)
        zp = jnp.where(row == 0, zprev_s[nb], pltpu.roll(z, 1, 0))
        zprev_s[nb] = z[Tv - 1:Tv, :]
        zs = z + (zp - z) * mu_ref[...]
        r = zs[:, 0:Wd]
        k = zs[:, Wd:2 * Wd]
        v = zs[:, 2 * Wd:3 * Wd]
        zwa = zs[:, 3 * Wd:3 * Wd + LORA_WA]
        zg = zs[:, 3 * Wd + LORA_WA:]
        if Tv < L:
            valid = row < Tv
            r = jnp.where(valid, r, 0.0)
            k = jnp.where(valid, k, 0.0)
            v = jnp.where(valid, v, 0.0)
        w_log = -_softplus(-(w0_ref[...] + _dot(jnp.tanh(zwa).astype(BF16), w2_ref[...]))) - 0.5
        ld = -jnp.exp(w_log)
        if Tv < L:
            ld = jnp.where(valid, ld, 0.0)
        iclr = _sigmoid(a0_ref[...] + _dot(zwa.astype(BF16), a2_ref[...]))
        g_all[nb] = _dot(_sigmoid(zg).astype(BF16), g2_ref[...])
        kkraw = k * kk_ref[...]
        kmod = k * (1.0 + (iclr - 1.0) * ka_ref[...])
        ld_hi = ld.astype(BF16)
        rem = ld - ld_hi.astype(F32)
        ld_mid = rem.astype(BF16)
        ld_lo = (rem - ld_mid.astype(F32)).astype(BF16)
        cum = _dot(tri, ld_hi) + _dot(tri, ld_mid) + _dot(tri, ld_lo)
        cum_l = cum[L - 1:L, :]
        cumm = cum - cum[L // 2 - 1:L // 2, :]
        e_m = jnp.exp(cumm)
        e_mi = jnp.exp(-cumm)
        e_mprev = jnp.exp(cumm - ld)
        e_0 = jnp.exp(cum)
        e_0prev = jnp.exp(cum - ld)
        e_end = jnp.exp(cum_l - cum)
        pl_all[nb] = jnp.exp(cum_l)
        for p in P:
            sl = sls[p]
            r_p, v_p, kmod_p, iclr_p = r[:, sl], v[:, sl], kmod[:, sl], iclr[:, sl]
            kkr = kkraw[:, sl]
            kk = kkr / jnp.maximum(jnp.sqrt(headsum(kkr * kkr)), 1e-12)
            a_p = -kk
            b_p = kk * iclr_p
            at = a_p * e_mprev[:, sl]
            rt = r_p * e_m[:, sl]
            bk = jnp.concatenate([b_p * e_mi[:, sl], kmod_p * e_mi[:, sl]], axis=0).astype(BF16)
            ar0 = jnp.concatenate([a_p * e_0prev[:, sl], r_p * e_0[:, sl]], axis=0).astype(BF16)
            be = jnp.concatenate([b_p * e_end[:, sl], kmod_p * e_end[:, sl]], axis=0).astype(BF16)
            pre[(nb, p)] = dict(r=r_p, v=v_p, kmod=kmod_p, at=at, rt=rt, bk=bk, ar0=ar0, be=be,
                                vb=v_p.astype(BF16))
    pairs = [(nb, p) for nb in range(NB) for p in P]
    wts = {q: wt_s[q[0] * RWKV_PAIRS + q[1]] for q in pairs}
    as0 = {q: _dot_nt(pre[q]["ar0"], wts[q].astype(BF16)) for q in pairs}
    heads = [(q, h) for q in pairs for h in range(2)]
    gms = {}
    for (q, h) in heads:
        mh = m0 if h == 0 else jnp.logical_not(m0)
        arm = jnp.concatenate([jnp.where(mh, pre[q]["at"], 0.0), jnp.where(mh, pre[q]["rt"], 0.0)],
                              axis=0)
        gms[(q, h)] = _dot_nt(arm.astype(BF16), pre[q]["bk"])
    nab, mm, ys = {}, {}, {}
    for (q, h) in heads:
        gm = gms[(q, h)]
        nab[(q, h)] = jnp.where(strict, gm[0:L, 0:L], 0.0).astype(BF16)
        n_ak = jnp.where(strict, gm[0:L, L:2 * L], 0.0)
        mm[(q, h)] = jnp.concatenate([jnp.where(incl, gm[L:2 * L, 0:L], 0.0),
                                      jnp.where(incl, gm[L:2 * L, L:2 * L], 0.0)],
                                     axis=1).astype(BF16)
        ys[(q, h)] = as0[q][0:L] + _dot(n_ak.astype(BF16), pre[q]["vb"])
    pws = dict(nab)
    nlev = int(math.log2(L))
    for lev in range(nlev):
        for hd in heads:
            ys[hd] = ys[hd] + _dot(pws[hd], ys[hd].astype(BF16))
        if lev + 1 < nlev:
            for hd in heads:
                pws[hd] = _dot(pws[hd], pws[hd]).astype(BF16)
    for q in pairs:
        nb, p = q
        sl = sls[p]
        pp = pre[q]
        u_p = jnp.where(m0, ys[(q, 0)], ys[(q, 1)])
        uv = jnp.concatenate([u_p, pp["v"]], axis=0).astype(BF16)
        o0 = as0[q][L:2 * L] + _dot(mm[(q, 0)], uv)
        o1 = as0[q][L:2 * L] + _dot(mm[(q, 1)], uv)
        o = jnp.where(m0, o0, o1)
        wt_new = wts[q] * pl_all[nb][:, sl] + jnp.where(blockdiag, _dot_tn(uv, pp["be"]), 0.0)
        wt_s[nb * RWKV_PAIRS + p] = wt_new
        wt_ref[nb * RWKV_PAIRS + p] = wt_new
        mean = headsum(o) * (1.0 / RWKV_HEAD)
        d = o - mean
        var = headsum(d * d) * (1.0 / RWKV_HEAD)
        on = d * lax.rsqrt(var + GN_EPS) * lnw_ref[:, sl] + lnb_ref[:, sl]
        bonus = headsum(pp["r"] * pp["kmod"] * rk_ref[:, sl]) * pp["v"]
        y = (on + bonus) * g_all[nb][:, sl]
        y_ref[:, nb * Wd + p * 128:nb * Wd + (p + 1) * 128] = y[0:Tv]


def _rwkv(zr, sh0, wt0, p, B, T):
    Tv = min(T, RWKV_CHUNK)
    NB = RWKV_NB
    kern = functools.partial(_rwkv_kernel, Tv=Tv, NB=NB)
    row = lambda n: pl.BlockSpec((1, n), lambda b, c: (0, 0))
    mat = lambda m, n: pl.BlockSpec((m, n), lambda b, c: (0, 0))
    st = pl.BlockSpec((NB * RWKV_PAIRS, 128, 128), lambda b, c: (b, 0, 0))
    return pl.pallas_call(
        kern,
        grid=(B // NB, T // Tv),
        in_specs=[
            pl.BlockSpec((Tv, NB * RWKV_COLS), lambda b, c: (c, b)),
            pl.BlockSpec((NB, 1, RWKV_COLS), lambda b, c: (b, 0, 0)),
            st,
            row(RWKV_COLS), row(RWKV_WIDTH), mat(LORA_WA, RWKV_WIDTH), row(RWKV_WIDTH),
            mat(LORA_WA, RWKV_WIDTH), mat(G_LORA, RWKV_WIDTH),
            row(RWKV_WIDTH), row(RWKV_WIDTH), row(RWKV_WIDTH), row(RWKV_WIDTH), row(RWKV_WIDTH),
        ],
        out_specs=[
            pl.BlockSpec((Tv, NB * RWKV_WIDTH), lambda b, c: (c, b)),
            st,
        ],
        out_shape=[
            jax.ShapeDtypeStruct((T, B * RWKV_WIDTH), F32),
            jax.ShapeDtypeStruct((B * RWKV_PAIRS, 128, 128), F32),
        ],
        scratch_shapes=[
            pltpu.VMEM((NB, 1, RWKV_COLS), F32),
            pltpu.VMEM((NB * RWKV_PAIRS, 128, 128), F32),
        ],
        compiler_params=_cparams(("arbitrary", "arbitrary")),
        name="rwkv7",
    )(zr, sh0, wt0, p["mu"], p["w0"], p["w2"], p["a0"], p["a2"], p["g2"],
      p["kk"], p["ka"], p["rk"], p["lnw"], p["lnb"])


def _outproj_kernel(x_ref, yl_ref, yr_ref, wa_ref, wb_ref, o_ref):
    o_ref[...] = (x_ref[...] + _dot(yl_ref[...].astype(BF16), wa_ref[...])
                  + _dot(yr_ref[...].astype(BF16), wb_ref[...]))


def _outproj(x, yl, yr, wa, wb, tm):
    B, T, D = x.shape
    return pl.pallas_call(
        _outproj_kernel,
        grid=(B, T // tm),
        in_specs=[
            pl.BlockSpec((None, tm, D), lambda b, i: (b, i, 0)),
            pl.BlockSpec((tm, LRU_WIDTH), lambda b, i: (i, b)),
            pl.BlockSpec((tm, RWKV_WIDTH), lambda b, i: (i, b)),
            pl.BlockSpec((LRU_WIDTH, D), lambda b, i: (0, 0)),
            pl.BlockSpec((RWKV_WIDTH, D), lambda b, i: (0, 0)),
        ],
        out_specs=pl.BlockSpec((None, tm, D), lambda b, i: (b, i, 0)),
        out_shape=jax.ShapeDtypeStruct((B, T, D), F32),
        compiler_params=_cparams(("parallel", "parallel")),
        name="outproj",
    )(x, yl, yr, wa, wb)


def _sort16_pairs():
    pairs = []

    def merge(lo, n, r):
        step = 2 * r
        if step < n:
            merge(lo, n, step)
            merge(lo + r, n, step)
            for i in range(lo + r, lo + n - r, step):
                pairs.append((i, i + r))
        else:
            pairs.append((lo, lo + r))

    def sort(lo, n):
        if n > 1:
            sort(lo, n // 2)
            sort(lo + n // 2, n // 2)
            merge(lo, n, 1)

    sort(0, 16)
    return pairs


_SORT16 = _sort16_pairs()


def _merge_top(g, n):
    g = list(g)
    S = g[0].shape[0]
    sub = lax.broadcasted_iota(jnp.int32, g[0].shape, 0).astype(F32)
    rows = []
    for it in range(n):
        m = jnp.max(g[0], axis=0, keepdims=True)
        rows.append(m)
        if it + 1 < n:
            first = jnp.min(jnp.where(g[0] == m, sub, float(S)), axis=0, keepdims=True)
            sel = sub == first
            for v in range(min(n - 1 - it, len(g) - 1)):
                g[v] = jnp.where(sel, g[v + 1], g[v])
            if len(g) <= n - 1 - it:
                g[-1] = jnp.where(sel, -jnp.inf, g[-1])
    return rows


def _top16_rows(x):
    g = [x[8 * v:8 * v + 8, :] for v in range(16)]
    for (i, j) in _SORT16:
        g[i], g[j] = jnp.maximum(g[i], g[j]), jnp.minimum(g[i], g[j])
    return _merge_top(g, PEER_TOPK)


def _gelu2(x):
    c = math.sqrt(2.0 / math.pi)
    inner = x * (c + (c * 0.044715) * (x * x))
    hx = 0.5 * x
    return hx + hx * jnp.tanh(inner)


def _peer_kernel(h_ref, g_ref, wq_ref, keys_ref, u_ref, vt_ref, o_ref,
                 xnt_s, s1_s, s2_s, tau_s, s1x_s, w_s, hid_s, acc_s, *, Tt, Ec):
    j = pl.program_id(1)
    nI = Ec // N_KEYS
    nT = Tt // 128
    Kk = PEER_TOPK
    log2e = 1.0 / math.log(2.0)

    @pl.when(j == 0)
    def _prep():
        xn = _rms(h_ref[...], g_ref[...])
        xnb = xn.astype(BF16)
        xnt_s[...] = xn.T.astype(BF16)
        acc_s[...] = jnp.zeros_like(acc_s)
        sub = lax.broadcasted_iota(jnp.int32, (Kk, Tt), 0)

        def cands(a1, t2):
            return [jnp.where(sub < Kk // (q + 1), a1 + t2[q], -jnp.inf) for q in range(Kk)]

        for h in range(PEER_HEADS):
            q = _dot_nt(wq_ref[h * 2 * PEER_HALF:(h + 1) * 2 * PEER_HALF, :], xnb)
            s1 = _dot(keys_ref[2 * h], q[0:PEER_HALF].astype(BF16))
            s2 = _dot(keys_ref[2 * h + 1], q[PEER_HALF:].astype(BF16))
            t1 = _top16_rows(s1)
            t2 = _top16_rows(s2)
            sc1 = (s1 - t1[0]) * log2e
            sc2 = (s2 - t2[0]) * log2e
            a1 = (jnp.concatenate(t1, axis=0) - t1[0]) * log2e
            a2 = [(t - t2[0]) * log2e for t in t2]
            zsum = jnp.zeros_like(t1[0])
            for b_ in _merge_top(cands(a1, a2), Kk):
                zsum = zsum + jnp.exp2(b_)
            lz = jnp.log2(zsum)
            sc2 = sc2 - lz
            a2 = [t - lz for t in a2]
            tau_s[h:h + 1, :] = _merge_top(cands(a1, a2), Kk)[Kk - 1]
            for tb in range(nT):
                cs = slice(tb * 128, (tb + 1) * 128)
                s1_s[h, tb] = sc1[:, cs]
                s2_s[h, tb] = sc2[:, cs]

    HK = N_KEYS // 2
    IG = 2
    r0d = pl.multiple_of(j * nI, nI)
    for h in range(PEER_HEADS):
        for tb in range(nT):
            s1x_s[h, tb] = s1_s[h, tb, pl.ds(r0d, nI), :]
    hid_s[...] = _dot(u_ref[...], xnt_s[...])
    for tb in range(nT):
        cs = slice(tb * 128, (tb + 1) * 128)
        for half in range(2):
            ks = slice(half * HK, (half + 1) * HK)
            for ig in range(0, nI, IG):
                accs = [jnp.zeros((HK, 128), F32) for _ in range(IG)]
                for h in range(PEER_HEADS):
                    s2h = s2_s[h, tb, ks, :]
                    tau = tau_s[h:h + 1, cs]
                    for ii in range(IG):
                        i1 = ig + ii
                        cval = s1x_s[h, tb, i1:i1 + 1, :] + s2h
                        accs[ii] = accs[ii] + jnp.where(cval >= tau, jnp.exp2(cval), 0.0)
                for ii in range(IG):
                    r0 = (ig + ii) * N_KEYS + half * HK
                    rs = slice(r0, r0 + HK)
                    w_s[rs, cs] = accs[ii].astype(BF16) * _gelu2(hid_s[rs, cs].astype(BF16))
    acc_s[...] += _dot(vt_ref[...], w_s[...])

    @pl.when(j == pl.num_programs(1) - 1)
    def _fin():
        o_ref[...] = h_ref[...] + acc_s[...].T


def _peer(h, g, wq_t, keys, u, vt, Tt, Ec):
    N, D = h.shape
    nT = Tt // 128
    nI = Ec // N_KEYS
    kern = functools.partial(_peer_kernel, Tt=Tt, Ec=Ec)
    score = pltpu.VMEM((PEER_HEADS, nT, N_KEYS, 128), F32)
    rows = pltpu.VMEM((PEER_HEADS, nT, nI, 128), F32)
    return pl.pallas_call(
        kern,
        grid=(N // Tt, N_EXPERTS // Ec),
        in_specs=[
            pl.BlockSpec((Tt, D), lambda i, j: (i, 0)),
            pl.BlockSpec((1, D), lambda i, j: (0, 0)),
            pl.BlockSpec((PEER_HEADS * 2 * PEER_HALF, D), lambda i, j: (0, 0)),
            pl.BlockSpec((2 * PEER_HEADS, N_KEYS, PEER_HALF), lambda i, j: (0, 0, 0)),
            pl.BlockSpec((Ec, D), lambda i, j: (j, 0)),
            pl.BlockSpec((D, Ec), lambda i, j: (0, j)),
        ],
        out_specs=pl.BlockSpec((Tt, D), lambda i, j: (i, 0)),
        out_shape=jax.ShapeDtypeStruct((N, D), F32),
        scratch_shapes=[
            pltpu.VMEM((D, Tt), BF16),
            score, score,
            pltpu.VMEM((PEER_HEADS, Tt), F32),
            rows,
            pltpu.VMEM((Ec, Tt), BF16),
            pltpu.VMEM((Ec, Tt), F32),
            pltpu.VMEM((D, Tt), F32),
        ],
        compiler_params=_cparams(("parallel", "arbitrary")),
        name="peer",
    )(h, g, wq_t, keys, u, vt)


def _ple_kernel(h_ref, p_ref, g_ref, pw_ref, gw_ref, fg_ref, o_ref, *, final):
    h = h_ref[...]
    gate = _sigmoid(_dot(_rms(h, g_ref[...]).astype(BF16), gw_ref[...]))
    out = h + _dot(p_ref[...].astype(BF16), pw_ref[...]) * gate
    if final:
        out = _rms(out, fg_ref[...])
    o_ref[...] = out


def _ple(h, p, g, pw, gw, fg, tm, final):
    N, D = h.shape
    return pl.pallas_call(
        functools.partial(_ple_kernel, final=final),
        grid=(N // tm,),
        in_specs=[
            pl.BlockSpec((tm, D), lambda i: (i, 0)),
            pl.BlockSpec((tm, D_PLE), lambda i: (i, 0)),
            pl.BlockSpec((1, D), lambda i: (0, 0)),
            pl.BlockSpec((D_PLE, D), lambda i: (0, 0)),
            pl.BlockSpec((D, D), lambda i: (0, 0)),
            pl.BlockSpec((1, D), lambda i: (0, 0)),
        ],
        out_specs=pl.BlockSpec((tm, D), lambda i: (i, 0)),
        out_shape=jax.ShapeDtypeStruct((N, D), F32),
        compiler_params=_cparams(("parallel",)),
        name="ple",
    )(h, p, g, pw, gw, fg)


def _layer_params(i, norm1_g, w_in, conv_w, conv_b, lru_wa, lru_ba, lru_wx, lru_bx, lru_lambda,
                  rwkv_mu, rwkv_w0, rwkv_w2, rwkv_a0, rwkv_a2, rwkv_g2, rwkv_kk, rwkv_ka, rwkv_rk,
                  rwkv_ln_w, rwkv_ln_b, w_out, norm2_g, peer_wq, peer_keys, peer_u, peer_v,
                  norm3_g, ple_w, ple_gate):
    row = lambda a: a.reshape(1, -1)
    wi = w_in[i].astype(BF16)
    zpad = jnp.zeros((LORA_WA // 2, RWKV_WIDTH), BF16)
    return dict(
        g1=row(norm1_g[i]), wl=wi[:, :LRU_COLS], wr=wi[:, LRU_COLS:],
        lru=dict(cw=conv_w[i], cb=row(conv_b[i]), wa=lru_wa[i].astype(BF16), ba=row(lru_ba[i]),
                 wx=lru_wx[i].astype(BF16), bx=row(lru_bx[i]), lam=row(lru_lambda[i])),
        rwkv=dict(mu=row(rwkv_mu[i]), w0=row(rwkv_w0[i]),
                  w2=jnp.concatenate([rwkv_w2[i].astype(BF16), zpad], axis=0), a0=row(rwkv_a0[i]),
                  a2=jnp.concatenate([zpad, rwkv_a2[i].astype(BF16)], axis=0),
                  g2=rwkv_g2[i].astype(BF16), kk=row(rwkv_kk[i]), ka=row(rwkv_ka[i]),
                  rk=row(rwkv_rk[i]), lnw=row(rwkv_ln_w[i]), lnb=row(rwkv_ln_b[i])),
        wo_a=w_out[i, :LRU_WIDTH].astype(BF16), wo_b=w_out[i, LRU_WIDTH:].astype(BF16),
        g2=row(norm2_g[i]), wq_t=peer_wq[i].T.astype(BF16),
        keys=peer_keys[i].reshape(2 * PEER_HEADS, N_KEYS, PEER_HALF).astype(BF16),
        u=peer_u[i].astype(BF16), vt=peer_v[i].T.astype(BF16),
        g3=row(norm3_g[i]), pw=ple_w[i].astype(BF16), gw=ple_gate[i].astype(BF16),
    )


def _pairs_from_state(s):
    B = s.shape[0]
    s = s.reshape(B, RWKV_PAIRS, 2, RWKV_HEAD, RWKV_HEAD)
    z = jnp.zeros_like(s[:, :, 0])
    top = jnp.concatenate([s[:, :, 0], z], axis=-1)
    bot = jnp.concatenate([z, s[:, :, 1]], axis=-1)
    return jnp.concatenate([top, bot], axis=-2)


def _state_from_pairs(w):
    B = w.shape[0]
    w = w.reshape(B, RWKV_PAIRS, 2, RWKV_HEAD, 2, RWKV_HEAD)
    return jnp.stack([w[:, :, 0, :, 0, :], w[:, :, 1, :, 1, :]], axis=2).reshape(
        B, RWKV_HEADS, RWKV_HEAD, RWKV_HEAD)


def _trunk(x, p, conv_st, lru_st, shift_st, wkv_st, first, layers, final_g, tiles):
    B, T, D = x.shape
    depth = len(layers)
    tm, lt = tiles
    tt, ec = PEER_TT, PEER_EC
    h = x
    convs, lrus, shifts, wkvs = [], [], [], []
    for i, lp in enumerate(layers):
        zl, zr = _inproj(h, lp["g1"], lp["wl"], lp["wr"], tm)
        tail0 = jnp.pad(jnp.swapaxes(conv_st[i], 0, 1),
                        ((8 - (CONV_W - 1), 0), (0, 0), (0, LRU_COLS - LRU_WIDTH))).reshape(8, B * LRU_COLS)
        y_lru, h_lru = _lru(zl, tail0, lru_st[i], lp["lru"], B, T, lt, first)
        y_rw, wt = _rwkv(zr, shift_st[i].reshape(B, 1, RWKV_COLS),
                         _pairs_from_state(wkv_st[i]).reshape(B * RWKV_PAIRS, 128, 128), lp["rwkv"], B, T)
        h1 = _outproj(h, y_lru, y_rw, lp["wo_a"], lp["wo_b"], tm)
        h2 = _peer(h1.reshape(B * T, D), lp["g2"], lp["wq_t"], lp["keys"], lp["u"], lp["vt"], tt, ec)
        h3 = _ple(h2, p[i].reshape(B * T, D_PLE), lp["g3"], lp["pw"], lp["gw"], final_g, tt,
                  final=(i == depth - 1))
        h = h3.reshape(B, T, D)
        ztail = zl[T - (CONV_W - 1):].reshape(CONV_W - 1, B, LRU_COLS)
        convs.append(jnp.swapaxes(ztail[:, :, :LRU_WIDTH], 0, 1))
        lrus.append(h_lru)
        shifts.append(zr[T - 1].reshape(B, RWKV_COLS))
        wkvs.append(_state_from_pairs(wt.reshape(B, RWKV_PAIRS, 128, 128)))
    return h, jnp.stack(convs), jnp.stack(lrus), jnp.stack(shifts), jnp.stack(wkvs)


def kernel(x_prompt, x_sample, p_prompt, p_sample, state_conv, state_lru, state_shift, state_wkv, norm1_g, w_in, conv_w, conv_b, lru_wa, lru_ba, lru_wx, lru_bx, lru_lambda, rwkv_mu, rwkv_w0, rwkv_w2, rwkv_a0, rwkv_a2, rwkv_g2, rwkv_kk, rwkv_ka, rwkv_rk, rwkv_ln_w, rwkv_ln_b, w_out, norm2_g, peer_wq, peer_keys, peer_u, peer_v, norm3_g, ple_w, ple_gate, final_g):
    depth = w_in.shape[0]
    layers = [_layer_params(i, norm1_g, w_in, conv_w, conv_b, lru_wa, lru_ba, lru_wx, lru_bx,
                            lru_lambda, rwkv_mu, rwkv_w0, rwkv_w2, rwkv_a0, rwkv_a2, rwkv_g2,
                            rwkv_kk, rwkv_ka, rwkv_rk, rwkv_ln_w, rwkv_ln_b, w_out, norm2_g,
                            peer_wq, peer_keys, peer_u, peer_v, norm3_g, ple_w, ple_gate)
              for i in range(depth)]
    fg = final_g.reshape(1, -1)
    bp = x_prompt.shape[0]
    dt = x_prompt.dtype
    zc = jnp.zeros((depth, bp, CONV_W - 1, LRU_WIDTH), dt)
    zl = jnp.zeros((depth, bp, LRU_WIDTH), dt)
    zs = jnp.zeros((depth, bp, RWKV_COLS), dt)
    zw = jnp.zeros((depth, bp, RWKV_HEADS, RWKV_HEAD, RWKV_HEAD), dt)
    tp = min(512, x_prompt.shape[1])
    ts = min(512, x_sample.shape[1])
    out_p = _trunk(x_prompt, p_prompt, zc, zl, zs, zw, True, layers, fg,
                   (tp, min(32, x_prompt.shape[1])))
    out_s = _trunk(x_sample, p_sample, state_conv, state_lru, state_shift, state_wkv, False,
                   layers, fg, (ts, min(32, x_sample.shape[1])))
    return (out_p[0], out_s[0]) + tuple(out_p[1:]) + tuple(out_s[1:])
```
